```python
import math
import jax, jax.numpy as jnp
from jax import lax
import numpy as np

D_MODEL = 1024
BATCH = 4
SEQ = 8192
DEPTH = 1

PLE_DIM = 256
D_FF = 2816
ATTN_WIDTH = 512
SSM_WIDTH = 512
N_ATTN_HEADS = 4
ATTN_HEAD_DIM = ATTN_WIDTH // (2 * N_ATTN_HEADS)
SSM_HEAD_DIM = 64
N_SSM_HEADS = SSM_WIDTH // SSM_HEAD_DIM
SSM_GROUPS = 2
SSM_STATE = 128
CONV_WIDTH = 4
CHUNK = 256
Q_BLOCK = 128
NORM_EPS = 1e-6
CONV_DIM = SSM_WIDTH + 2 * SSM_GROUPS * SSM_STATE
IN_PROJ_DIM = 3 * ATTN_WIDTH + SSM_WIDTH + CONV_DIM + N_SSM_HEADS

kernel_name = 'hymba_diffattn_ssd_macaron_alibi_ple'


def rms_norm(x, g):
    xf = x.astype(jnp.float32)
    y = xf * lax.rsqrt(jnp.mean(xf * xf, axis=-1, keepdims=True) + NORM_EPS)
    return (y * g.astype(jnp.float32)).astype(x.dtype)


def swiglu(h, w_gate, w_up, w_down):
    return (jax.nn.silu(h @ w_gate) * (h @ w_up)) @ w_down


def causal_depthwise_conv(u, w, b):
    c = u.shape[-1]
    out = lax.conv_general_dilated(
        u, w.astype(u.dtype)[:, None, :], window_strides=(1,),
        padding=[(CONV_WIDTH - 1, 0)], dimension_numbers=('NWC', 'WIO', 'NWC'),
        feature_group_count=c)
    return out + b.astype(u.dtype)


def alibi_slopes():
    h = jnp.arange(1, N_ATTN_HEADS + 1, dtype=jnp.float32)
    return jnp.exp2(-8.0 * h / N_ATTN_HEADS)


def differential_attention(q, k, v, lam):
    bsz, s = q.shape[:2]
    nb = s // Q_BLOCK
    qb = jnp.moveaxis(q.reshape(bsz, nb, Q_BLOCK, N_ATTN_HEADS, 2, ATTN_HEAD_DIM), 1, 0)
    kpos = jnp.arange(s)
    slopes = alibi_slopes()
    scale = ATTN_HEAD_DIM ** -0.5

    def query_block(args):
        q_blk, bi = args
        qpos = bi * Q_BLOCK + jnp.arange(Q_BLOCK)
        dist = qpos[:, None] - kpos[None, :]
        bias = -slopes[:, None, None] * dist.astype(jnp.float32)[None]
        scores = jnp.einsum('bqhcd,bkhcd->bhcqk', q_blk, k).astype(jnp.float32) * scale
        scores = scores + bias[None, :, None]
        scores = jnp.where((dist >= 0)[None, None, None], scores, -jnp.inf)
        probs = jax.nn.softmax(scores, axis=-1)
        diff = probs[:, :, 0] - lam * probs[:, :, 1]
        return jnp.einsum('bhqk,bkhe->bqhe', diff.astype(v.dtype), v)

    out = lax.map(query_block, (qb, jnp.arange(nb)))
    return jnp.moveaxis(out, 0, 1).reshape(bsz, s, N_ATTN_HEADS, 2 * ATTN_HEAD_DIM)


def ssd_chunked_scan(x, dt, a, b_in, c_in):
    bsz, length = x.shape[:2]
    pad = (-length) % CHUNK
    x = jnp.pad(x, ((0, 0), (0, pad), (0, 0), (0, 0)))
    dt = jnp.pad(dt, ((0, 0), (0, pad), (0, 0)))
    b_in = jnp.pad(b_in, ((0, 0), (0, pad), (0, 0), (0, 0)))
    c_in = jnp.pad(c_in, ((0, 0), (0, pad), (0, 0), (0, 0)))
    lp = length + pad
    nc = lp // CHUNK
    r = N_SSM_HEADS // SSM_GROUPS
    xd = (x * dt[..., None]).reshape(bsz, nc, CHUNK, SSM_GROUPS, r, SSM_HEAD_DIM)
    ad = (dt * a).reshape(bsz, nc, CHUNK, SSM_GROUPS, r)
    bc = b_in.reshape(bsz, nc, CHUNK, SSM_GROUPS, SSM_STATE)
    cc = c_in.reshape(bsz, nc, CHUNK, SSM_GROUPS, SSM_STATE)
    a_cs = jnp.cumsum(ad, axis=2)
    causal = jnp.tril(jnp.ones((CHUNK, CHUNK), dtype=bool))
    seg = a_cs[:, :, :, None] - a_cs[:, :, None, :]
    decay = jnp.exp(jnp.where(causal[None, None, :, :, None, None], seg, -jnp.inf))
    cb = jnp.einsum('bclgn,bcsgn->bclsg', cc, bc)
    y_diag = jnp.einsum('bclsg,bclsgr,bcsgrp->bclgrp', cb, decay, xd)
    decay_to_end = jnp.exp(a_cs[:, :, -1:] - a_cs)
    chunk_states = jnp.einsum('bclgn,bclgr,bclgrp->bcgrpn', bc, decay_to_end, xd)
    chunk_decay = jnp.exp(a_cs[:, :, -1])

    def carry_state(h, inp):
        st, dec = inp
        return h * dec[..., None, None] + st, h

    h0 = jnp.zeros((bsz, SSM_GROUPS, r, SSM_HEAD_DIM, SSM_STATE), xd.dtype)
    _, h_in = lax.scan(carry_state, h0,
                       (jnp.moveaxis(chunk_states, 1, 0), jnp.moveaxis(chunk_decay, 1, 0)))
    h_in = jnp.moveaxis(h_in, 0, 1)
    y_off = jnp.einsum('bclgn,bcgrpn,bclgr->bclgrp', cc, h_in, jnp.exp(a_cs))
    y = (y_diag + y_off).reshape(bsz, lp, N_SSM_HEADS, SSM_HEAD_DIM)
    return y[:, :length]


def setup_inputs(seed: int = 0) -> dict:
    key = jax.random.key(seed)
    ks = jax.random.split(key, 32)
    f32 = jnp.float32

    def dense(k, fan_in, fan_out):
        return jax.random.normal(k, (DEPTH, fan_in, fan_out), f32) * fan_in ** -0.5

    def gain(k, n):
        return 1.0 + 0.02 * jax.random.normal(k, (DEPTH, n), f32)

    u = jax.random.uniform(ks[20], (DEPTH, N_SSM_HEADS), f32)
    dt0 = jnp.exp(u * (math.log(0.1) - math.log(0.001)) + math.log(0.001))
    dt_bias = dt0 + jnp.log(-jnp.expm1(-dt0))
    a_log = jnp.log(jax.random.uniform(ks[21], (DEPTH, N_SSM_HEADS), f32, 1.0, 16.0))
    return {
        'x': jax.random.normal(ks[0], (BATCH, SEQ, D_MODEL), f32),
        'p': jax.random.normal(ks[1], (DEPTH, BATCH, SEQ, PLE_DIM), f32),
        'ffn1_norm': gain(ks[2], D_MODEL),
        'ffn1_w_gate': dense(ks[3], D_MODEL, D_FF),
        'ffn1_w_up': dense(ks[4], D_MODEL, D_FF),
        'ffn1_w_down': dense(ks[5], D_FF, D_MODEL),
        'mix_norm': gain(ks[6], D_MODEL),
        'w_in': dense(ks[7], D_MODEL, IN_PROJ_DIM),
        'q_norm': gain(ks[8], ATTN_HEAD_DIM),
        'k_norm': gain(ks[9], ATTN_HEAD_DIM),
        'lambda_q1': 0.1 * jax.random.normal(ks[10], (DEPTH, ATTN_HEAD_DIM), f32),
        'lambda_k1': 0.1 * jax.random.normal(ks[11], (DEPTH, ATTN_HEAD_DIM), f32),
        'lambda_q2': 0.1 * jax.random.normal(ks[12], (DEPTH, ATTN_HEAD_DIM), f32),
        'lambda_k2': 0.1 * jax.random.normal(ks[13], (DEPTH, ATTN_HEAD_DIM), f32),
        'attn_out_norm': gain(ks[14], 2 * ATTN_HEAD_DIM),
        'conv_w': jax.random.normal(ks[15], (DEPTH, CONV_WIDTH, CONV_DIM), f32) * CONV_WIDTH ** -0.5,
        'conv_b': 0.02 * jax.random.normal(ks[16], (DEPTH, CONV_DIM), f32),
        'dt_bias': dt_bias,
        'a_log': a_log,
        'd_skip': gain(ks[17], N_SSM_HEADS),
        'ssm_out_norm': gain(ks[18], SSM_WIDTH),
        'w_out': dense(ks[19], ATTN_WIDTH + SSM_WIDTH, D_MODEL),
        'ffn2_norm': gain(ks[22], D_MODEL),
        'ffn2_w_gate': dense(ks[23], D_MODEL, D_FF),
        'ffn2_w_up': dense(ks[24], D_MODEL, D_FF),
        'ffn2_w_down': dense(ks[25], D_FF, D_MODEL),
        'ple_gate_norm': gain(ks[26], D_MODEL),
        'w_ple_gate': dense(ks[27], D_MODEL, D_MODEL),
        'w_ple_proj': dense(ks[28], PLE_DIM, D_MODEL),
        'ple_norm': gain(ks[29], D_MODEL),
    }


def reference(x, p, ffn1_norm, ffn1_w_gate, ffn1_w_up, ffn1_w_down, mix_norm, w_in,
              q_norm, k_norm, lambda_q1, lambda_k1, lambda_q2, lambda_k2, attn_out_norm,
              conv_w, conv_b, dt_bias, a_log, d_skip, ssm_out_norm, w_out,
              ffn2_norm, ffn2_w_gate, ffn2_w_up, ffn2_w_down,
              ple_gate_norm, w_ple_gate, w_ple_proj, ple_norm):
    bsz, s, _ = x.shape
    splits = np.cumsum([ATTN_WIDTH, ATTN_WIDTH, ATTN_WIDTH, SSM_WIDTH, CONV_DIM]).tolist()
    for i in range(DEPTH):
        lam_init = 0.8 - 0.6 * math.exp(-0.3 * i)
        x = x + 0.5 * swiglu(rms_norm(x, ffn1_norm[i]), ffn1_w_gate[i], ffn1_w_up[i], ffn1_w_down[i])

        h = rms_norm(x, mix_norm[i])
        u = h @ w_in[i]
        q, k, v, z, xbc, dt_raw = jnp.split(u, splits, axis=-1)

        q = rms_norm(q.reshape(bsz, s, N_ATTN_HEADS, 2, ATTN_HEAD_DIM), q_norm[i])
        k = rms_norm(k.reshape(bsz, s, N_ATTN_HEADS, 2, ATTN_HEAD_DIM), k_norm[i])
        v = v.reshape(bsz, s, N_ATTN_HEADS, 2 * ATTN_HEAD_DIM)
        lam = (jnp.exp(jnp.sum(lambda_q1[i].astype(jnp.float32) * lambda_k1[i].astype(jnp.float32)))
               - jnp.exp(jnp.sum(lambda_q2[i].astype(jnp.float32) * lambda_k2[i].astype(jnp.float32)))
               + lam_init)
        o_attn = differential_attention(q, k, v, lam)
        o_attn = (rms_norm(o_attn, attn_out_norm[i]) * (1.0 - lam_init)).reshape(bsz, s, ATTN_WIDTH)

        xbc = jax.nn.silu(causal_depthwise_conv(xbc, conv_w[i], conv_b[i]))
        xs, b_ssm, c_ssm = jnp.split(xbc, [SSM_WIDTH, SSM_WIDTH + SSM_GROUPS * SSM_STATE], axis=-1)
        xs = xs.reshape(bsz, s, N_SSM_HEADS, SSM_HEAD_DIM).astype(jnp.float32)
        b_ssm = b_ssm.reshape(bsz, s, SSM_GROUPS, SSM_STATE).astype(jnp.float32)
        c_ssm = c_ssm.reshape(bsz, s, SSM_GROUPS, SSM_STATE).astype(jnp.float32)
        dt = jax.nn.softplus((dt_raw + dt_bias[i]).astype(jnp.float32))
        a = -jnp.exp(a_log[i].astype(jnp.float32))
        y = ssd_chunked_scan(xs, dt, a, b_ssm, c_ssm) + d_skip[i].astype(jnp.float32)[:, None] * xs
        y = y.reshape(bsz, s, SSM_WIDTH).astype(x.dtype) * jax.nn.silu(z)
        gsz = SSM_WIDTH // SSM_GROUPS
        y = rms_norm(y.reshape(bsz, s, SSM_GROUPS, gsz),
                     ssm_out_norm[i].reshape(SSM_GROUPS, gsz)).reshape(bsz, s, SSM_WIDTH)

        x = x + jnp.concatenate([o_attn, y], axis=-1) @ w_out[i]

        x = x + 0.5 * swiglu(rms_norm(x, ffn2_norm[i]), ffn2_w_gate[i], ffn2_w_up[i], ffn2_w_down[i])

        e = rms_norm(p[i] @ w_ple_proj[i], ple_norm[i])
        gate = jax.nn.sigmoid(rms_norm(x, ple_gate_norm[i]) @ w_ple_gate[i])
        x = x + gate * e
    return x
```

```python
import functools
import math

import jax
import jax.numpy as jnp
from jax import lax
from jax.experimental import pallas as pl
from jax.experimental.pallas import tpu as pltpu

N_ATTN_HEADS = 4
ATTN_HEAD_DIM = 64
SSM_HEAD_DIM = 64
N_SSM_HEADS = 8
SSM_GROUPS = 2
SSM_STATE = 128
CONV_WIDTH = 4
CHUNK = 256
NORM_EPS = 1e-6

LANES = 128
SUBLANES = 8
VMEM_LIMIT_BYTES = 56 * 1024 * 1024

ATTN_WIDTH = 2 * N_ATTN_HEADS * ATTN_HEAD_DIM
SSM_WIDTH = N_SSM_HEADS * SSM_HEAD_DIM
HEADS_PER_GROUP = N_SSM_HEADS // SSM_GROUPS
GROUP_WIDTH = HEADS_PER_GROUP * SSM_HEAD_DIM
CONV_DIM = SSM_WIDTH + 2 * SSM_GROUPS * SSM_STATE
NEG_BIG = -1e30

BF16 = jnp.bfloat16
F32 = jnp.float32


def _rms(x, g):
    ms = jnp.mean(x * x, axis=-1, keepdims=True)
    return x * lax.rsqrt(ms + NORM_EPS) * g


def _silu(x):
    return x * jax.nn.sigmoid(x)


def _dot(a, b):
    return jnp.dot(a, b, preferred_element_type=F32)


def _dot_nt(a, b):
    return lax.dot_general(a, b, (((1,), (1,)), ((), ())), preferred_element_type=F32)


def _dot_tn(a, b):
    return lax.dot_general(a, b, (((0,), (0,)), ((), ())), preferred_element_type=F32)


def _resident(shape):
    zeros = (0,) * len(shape)
    return pl.BlockSpec(shape, lambda *_: zeros, pipeline_mode=pl.Buffered(1))


def _swiglu_residual(x, norm, wg_ref, wu_ref, wd_ref, n_ff_chunks):
    h = _rms(x, norm).astype(BF16)
    d_ff = wg_ref.shape[1]
    cf = d_ff // n_ff_chunks
    acc = None
    for c in range(n_ff_chunks):
        g = _dot(h, wg_ref[:, c * cf:(c + 1) * cf])
        u = _dot(h, wu_ref[:, c * cf:(c + 1) * cf])
        a = (_silu(g) * u).astype(BF16)
        d = _dot(a, wd_ref[c * cf:(c + 1) * cf, :])
        acc = d if acc is None else acc + d
    return x + 0.5 * acc


def _pair_rms(x, g):
    lane = lax.broadcasted_iota(jnp.int32, x.shape, 1)
    lo = lane < ATTN_HEAD_DIM
    sq = x * x
    s_all = jnp.sum(sq, axis=-1, keepdims=True)
    s_lo = jnp.sum(jnp.where(lo, sq, 0.0), axis=-1, keepdims=True)
    s_hi = s_all - s_lo
    inv = 1.0 / ATTN_HEAD_DIM
    r = jnp.where(lo, lax.rsqrt(s_lo * inv + NORM_EPS), lax.rsqrt(s_hi * inv + NORM_EPS))
    return x * r * g


def _ffn_inproj_kernel(x_ref, n1_ref, wg_ref, wu_ref, wd_ref, nm_ref, win_ref, wdt_ref,
                       qg_ref, kg_ref, dtb_ref,
                       x1_ref, q_ref, k_ref, v_ref, z_ref, xbc_ref, dt_ref, *, n_ff_chunks):
    x1 = _swiglu_residual(x_ref[...], n1_ref[...], wg_ref, wu_ref, wd_ref, n_ff_chunks)
    x1_ref[...] = x1
    h = _rms(x1, nm_ref[...]).astype(BF16)
    aw = ATTN_WIDTH
    for hd in range(N_ATTN_HEADS):
        sl = slice(hd * LANES, (hd + 1) * LANES)
        qh = _dot(h, win_ref[:, hd * LANES:(hd + 1) * LANES])
        kh = _dot(h, win_ref[:, aw + hd * LANES:aw + (hd + 1) * LANES])
        q_ref[:, sl] = _pair_rms(qh, qg_ref[...]).astype(BF16)
        k_ref[:, sl] = _pair_rms(kh, kg_ref[...]).astype(BF16)
    v_ref[...] = _dot(h, win_ref[:, 2 * aw:3 * aw]).astype(BF16)
    z_ref[...] = _dot(h, win_ref[:, 3 * aw:3 * aw + SSM_WIDTH])
    xbc_ref[...] = _dot(h, win_ref[:, 3 * aw + SSM_WIDTH:])
    dt_ref[...] = _dot(h, wdt_ref[...]) + dtb_ref[...]


def _ffn_inproj(x, n1, wg, wu, wd, nm, win, wdt, qg, kg, dtb, *, tm, n_ff_chunks):
    n, d = x.shape
    d_ff = wg.shape[1]
    row = lambda w: pl.BlockSpec((tm, w), lambda i: (i, 0))
    out_shape = (
        jax.ShapeDtypeStruct((n, d), F32),
        jax.ShapeDtypeStruct((n, ATTN_WIDTH), BF16),
        jax.ShapeDtypeStruct((n, ATTN_WIDTH), BF16),
        jax.ShapeDtypeStruct((n, ATTN_WIDTH), BF16),
        jax.ShapeDtypeStruct((n, SSM_WIDTH), F32),
        jax.ShapeDtypeStruct((n, CONV_DIM), F32),
        jax.ShapeDtypeStruct((n, LANES), F32),
    )
    return pl.pallas_call(
        functools.partial(_ffn_inproj_kernel, n_ff_chunks=n_ff_chunks),
        grid=(n // tm,),
        in_specs=[row(d), _resident((1, d)), _resident((d, d_ff)), _resident((d, d_ff)),
                  _resident((d_ff, d)), _resident((1, d)), _resident(win.shape), _resident(wdt.shape),
                  _resident((1, LANES)), _resident((1, LANES)), _resident((1, LANES))],
        out_specs=(row(d), row(ATTN_WIDTH), row(ATTN_WIDTH), row(ATTN_WIDTH), row(SSM_WIDTH),
                   row(CONV_DIM), row(LANES)),
        out_shape=out_shape,
        compiler_params=pltpu.CompilerParams(dimension_semantics=("arbitrary",),
                                             vmem_limit_bytes=VMEM_LIMIT_BYTES),
        name="ffn_inproj",
    )(x, n1, wg, wu, wd, nm, win, wdt, qg, kg, dtb)


def _diff_attn_kernel(q_ref, k_ref, v_ref, lq1_ref, lk1_ref, lq2_ref, lk2_ref, on_ref, o_ref,
                      acc_ref, *, tq, lam_init):
    hd = pl.program_id(1)
    qi = pl.program_id(2)
    tk = tq
    q = q_ref[0]
    lane = lax.broadcasted_iota(jnp.int32, q.shape, 1)
    zero = jnp.zeros_like(q)
    qs = (jnp.where(lane < ATTN_HEAD_DIM, q, zero), jnp.where(lane >= ATTN_HEAD_DIM, q, zero))

    slope_bits = (127 - (8 // N_ATTN_HEADS) * (hd + 1)) << 23
    slope = lax.bitcast_convert_type(jnp.full((1, tk), slope_bits, jnp.int32), F32)
    kcol = lax.broadcasted_iota(jnp.int32, (1, tk), 1)
    row = lax.broadcasted_iota(jnp.int32, (tq, tk), 0)
    col = lax.broadcasted_iota(jnp.int32, (tq, tk), 1)
    causal = row >= col

    acc_ref[...] = jnp.zeros_like(acc_ref)

    def step(j, carry, masked):
        start = pl.multiple_of(j * tk, tk)
        kb = k_ref[0, pl.ds(start, tk), :]
        vb = v_ref[0, pl.ds(start, tk), :]
        bias = slope * ((j - qi) * tk + kcol).astype(F32)
        new = []
        for c in range(2):
            m, l = carry[2 * c], carry[2 * c + 1]
            s = _dot_nt(qs[c], kb) + bias
            if masked:
                s = jnp.where(causal, s, NEG_BIG)
            m_new = jnp.maximum(m, jnp.max(s, axis=-1, keepdims=True))
            p = jnp.exp(s - m_new)
            alpha = jnp.exp(m - m_new)
            l_new = alpha * l + jnp.sum(p, axis=-1, keepdims=True)
            acc_ref[c] = alpha * acc_ref[c] + _dot(p.astype(BF16), vb)
            new += [m_new, l_new]
        return tuple(new)

    m0 = jnp.full((tq, 1), NEG_BIG, F32)
    l0 = jnp.zeros((tq, 1), F32)
    carry = lax.fori_loop(0, qi, lambda j, c: step(j, c, False), (m0, l0, m0, l0))
    _, l1, _, l2 = step(qi, carry, True)

    lam = (jnp.exp(jnp.sum(lq1_ref[...] * lk1_ref[...], axis=-1, keepdims=True))
           - jnp.exp(jnp.sum(lq2_ref[...] * lk2_ref[...], axis=-1, keepdims=True)) + lam_init)
    o = acc_ref[0] / l1 - lam * (acc_ref[1] / l2)
    o_ref[0] = (_rms(o, on_ref[...]) * (1.0 - lam_init)).astype(BF16)


def _diff_attn(q, k, v, lq1, lk1, lq2, lk2, onorm, *, tq, lam_init):
    b, s, _ = q.shape
    qspec = pl.BlockSpec((1, tq, LANES), lambda bi, hi, qi: (bi, qi, hi))
    kvspec = pl.BlockSpec((1, s, LANES), lambda bi, hi, qi: (bi, 0, hi))
    small = lambda w: pl.BlockSpec((1, w), lambda bi, hi, qi: (0, 0))
    return pl.pallas_call(
        functools.partial(_diff_attn_kernel, tq=tq, lam_init=lam_init),
        grid=(b, N_ATTN_HEADS, s // tq),
        in_specs=[qspec, kvspec, kvspec, small(ATTN_HEAD_DIM), small(ATTN_HEAD_DIM),
                  small(ATTN_HEAD_DIM), small(ATTN_HEAD_DIM), small(LANES)],
        out_specs=qspec,
        out_shape=jax.ShapeDtypeStruct((b, s, ATTN_WIDTH), BF16),
        scratch_shapes=[pltpu.VMEM((2, tq, LANES), F32)],
        compiler_params=pltpu.CompilerParams(
            dimension_semantics=("arbitrary", "arbitrary", "arbitrary"),
            vmem_limit_bytes=VMEM_LIMIT_BYTES),
        name="diff_attn",
    )(q, k, v, lq1, lk1, lq2, lk2, onorm)


def _split3(x):
    hi = x.astype(BF16)
    r = x - hi.astype(F32)
    mid = r.astype(BF16)
    lo = (r - mid.astype(F32)).astype(BF16)
    return hi, mid, lo


def _expand_heads(cols):
    rows = cols.shape[0]
    lane = lax.broadcasted_iota(jnp.int32, (rows, LANES), 1)
    parts = []
    for pr in range(N_SSM_HEADS // 2):
        a = jnp.broadcast_to(cols[:, 2 * pr:2 * pr + 1], (rows, LANES))
        b = jnp.broadcast_to(cols[:, 2 * pr + 1:2 * pr + 2], (rows, LANES))
        parts.append(jnp.where(lane < SSM_HEAD_DIM, a, b))
    return jnp.concatenate(parts, axis=1)


def _ssd_kernel(xbc_ref, z_ref, dt_ref, cw_ref, cb_ref, alog_ref, dskip_ref, on_ref, y_ref,
                xpad_ref, h_ref):
    c = pl.program_id(1)
    L = CHUNK
    halo = SUBLANES

    @pl.when(c == 0)
    def _():
        xpad_ref[0:halo, :] = jnp.zeros((halo, CONV_DIM), F32)
        h_ref[...] = jnp.zeros_like(h_ref)

    cur = xbc_ref[0]
    xpad_ref[halo:halo + L, :] = cur
    conv = cur * cw_ref[CONV_WIDTH - 1:CONV_WIDTH, :] + cb_ref[...]
    for j in range(CONV_WIDTH - 1):
        off = halo - (CONV_WIDTH - 1) + j
        conv = conv + xpad_ref[off:off + L, :] * cw_ref[j:j + 1, :]
    xpad_ref[0:halo, :] = cur[L - halo:, :]
    act = _silu(conv)
    xs = act[:, :SSM_WIDTH]

    dt = jax.nn.softplus(dt_ref[0])
    a = -jnp.exp(alog_ref[...])
    ad = dt * a

    r = lax.broadcasted_iota(jnp.int32, (L, L), 0)
    cc = lax.broadcasted_iota(jnp.int32, (L, L), 1)
    tril = r >= cc
    tri = jnp.where(tril, 1.0, 0.0).astype(BF16)
    hi, mid, lo = _split3(ad)
    a_cs = _dot(tri, hi) + _dot(tri, mid) + _dot(tri, lo)
    a_cs_t = a_cs.T

    dt_l = _expand_heads(dt)
    acs_l = _expand_heads(a_cs)
    last_l = acs_l[L - 1:L, :]
    xd = xs * dt_l
    xw = (xd * jnp.exp(last_l - acs_l)).astype(BF16)
    e_in = jnp.exp(acs_l)
    e_chunk_t = jnp.exp(a_cs_t[:, L - 1:L])
    xd16 = xd.astype(BF16)
    glane = lax.broadcasted_iota(jnp.int32, (L, GROUP_WIDTH), 1)

    ys = []
    for g in range(SSM_GROUPS):
        bg = act[:, SSM_WIDTH + g * SSM_STATE:SSM_WIDTH + (g + 1) * SSM_STATE].astype(BF16)
        cg = act[:, SSM_WIDTH + (SSM_GROUPS + g) * SSM_STATE:
                 SSM_WIDTH + (SSM_GROUPS + g + 1) * SSM_STATE].astype(BF16)
        gs = slice(g * GROUP_WIDTH, (g + 1) * GROUP_WIDTH)
        cb = _dot_nt(cg, bg)
        xd_g = xd16[:, gs]
        y_g = None
        for hl in range(HEADS_PER_GROUP):
            hd = g * HEADS_PER_GROUP + hl
            seg = a_cs[:, hd:hd + 1] - a_cs_t[hd:hd + 1, :]
            m = (cb * jnp.exp(jnp.where(tril, seg, NEG_BIG))).astype(BF16)
            in_head = (glane >= hl * SSM_HEAD_DIM) & (glane < (hl + 1) * SSM_HEAD_DIM)
            d = _dot(m, jnp.where(in_head, xd_g, jnp.zeros_like(xd_g)))
            y_g = d if y_g is None else y_g + d
        h_in = h_ref[g]
        y_off = _dot_nt(cg, h_in.astype(BF16)) * e_in[:, gs]
        ys.append(y_g + y_off)
        decay_rows = jnp.concatenate(
            [jnp.broadcast_to(e_chunk_t[g * HEADS_PER_GROUP + hl:g * HEADS_PER_GROUP + hl + 1, :],
                              (SSM_HEAD_DIM, SSM_STATE)) for hl in range(HEADS_PER_GROUP)], axis=0)
        h_ref[g] = h_in * decay_rows + _dot_tn(xw[:, gs], bg)

    y = jnp.concatenate(ys, axis=1) + dskip_ref[...] * xs
    y = y * _silu(z_ref[0])
    outs = []
    for g in range(SSM_GROUPS):
        gs = slice(g * GROUP_WIDTH, (g + 1) * GROUP_WIDTH)
        outs.append(_rms(y[:, gs], on_ref[:, gs]))
    y_ref[0] = jnp.concatenate(outs, axis=1).astype(BF16)


def _ssd(xbc, z, dt, cw, cb, alog, dskip, onorm):
    b, s, _ = xbc.shape
    blk = lambda w: pl.BlockSpec((1, CHUNK, w), lambda bi, ci: (bi, ci, 0))
    small = lambda r, w: pl.BlockSpec((r, w), lambda bi, ci: (0, 0))
    return pl.pallas_call(
        _ssd_kernel,
        grid=(b, s // CHUNK),
        in_specs=[blk(CONV_DIM), blk(SSM_WIDTH), blk(LANES), small(CONV_WIDTH, CONV_DIM),
                  small(1, CONV_DIM), small(1, LANES), small(1, SSM_WIDTH), small(1, SSM_WIDTH)],
        out_specs=blk(SSM_WIDTH),
        out_shape=jax.ShapeDtypeStruct((b, s, SSM_WIDTH), BF16),
        scratch_shapes=[pltpu.VMEM((SUBLANES + CHUNK, CONV_DIM), F32),
                        pltpu.VMEM((SSM_GROUPS, GROUP_WIDTH, SSM_STATE), F32)],
        compiler_params=pltpu.CompilerParams(dimension_semantics=("arbitrary", "arbitrary"),
                                             vmem_limit_bytes=VMEM_LIMIT_BYTES),
        name="ssd",
    )(xbc, z, dt, cw, cb, alog, dskip, onorm)


def _out_ffn_ple_kernel(x1_ref, oa_ref, ys_ref, p_ref, woa_ref, woy_ref, n2_ref, wg_ref, wu_ref,
                        wd_ref, ng_ref, wpg_ref, wpp_ref, npl_ref, out_ref, *, n_ff_chunks):
    x2 = x1_ref[...] + _dot(oa_ref[...], woa_ref[...]) + _dot(ys_ref[...], woy_ref[...])
    x3 = _swiglu_residual(x2, n2_ref[...], wg_ref, wu_ref, wd_ref, n_ff_chunks)
    e = _rms(_dot(p_ref[...].astype(BF16), wpp_ref[...]), npl_ref[...])
    gate = jax.nn.sigmoid(_dot(_rms(x3, ng_ref[...]).astype(BF16), wpg_ref[...]))
    out_ref[...] = x3 + gate * e


def _out_ffn_ple(x1, oa, ys, p, woa, woy, n2, wg, wu, wd, ng, wpg, wpp, npl, *, tm, n_ff_chunks):
    n, d = x1.shape
    d_ff = wg.shape[1]
    row = lambda w: pl.BlockSpec((tm, w), lambda i: (i, 0))
    return pl.pallas_call(
        functools.partial(_out_ffn_ple_kernel, n_ff_chunks=n_ff_chunks),
        grid=(n // tm,),
        in_specs=[row(d), row(ATTN_WIDTH), row(SSM_WIDTH), row(p.shape[1]),
                  _resident(woa.shape), _resident(woy.shape), _resident((1, d)),
                  _resident((d, d_ff)), _resident((d, d_ff)), _resident((d_ff, d)),
                  _resident((1, d)), _resident(wpg.shape), _resident(wpp.shape), _resident((1, d))],
        out_specs=row(d),
        out_shape=jax.ShapeDtypeStruct((n, d), F32),
        compiler_params=pltpu.CompilerParams(dimension_semantics=("arbitrary",),
                                             vmem_limit_bytes=VMEM_LIMIT_BYTES),
        name="out_ffn_ple",
    )(x1, oa, ys, p, woa, woy, n2, wg, wu, wd, ng, wpg, wpp, npl)


def _tiles(n, s, d_ff):
    tm = 256 if n % 256 == 0 else SUBLANES
    tq = 512 if s % 512 == 0 else CHUNK
    n_ff_chunks = 2 if d_ff % (2 * LANES) == 0 else 1
    return tm, tq, n_ff_chunks


def _layer(i, x, p_i, prm):
    b, s, d = x.shape
    n = b * s
    d_ff = prm["ffn1_w_gate"].shape[-1]
    tm, tq, n_ff_chunks = _tiles(n, s, d_ff)
    lam_init = 0.8 - 0.6 * math.exp(-0.3 * i)
    row = lambda v: v.reshape(1, -1).astype(F32)
    w16 = lambda w: w.astype(BF16)

    w_in = prm["w_in"]
    n_main = 3 * ATTN_WIDTH + SSM_WIDTH + CONV_DIM
    win = w16(w_in[:, :n_main])
    wdt = w16(jnp.pad(w_in[:, n_main:], ((0, 0), (0, LANES - N_SSM_HEADS))))
    dtb = jnp.pad(row(prm["dt_bias"]), ((0, 0), (0, LANES - N_SSM_HEADS)))
    alog = jnp.pad(row(prm["a_log"]), ((0, 0), (0, LANES - N_SSM_HEADS)))
    qg = jnp.tile(row(prm["q_norm"]), (1, 2)) * (ATTN_HEAD_DIM ** -0.5)
    kg = jnp.tile(row(prm["k_norm"]), (1, 2))
    dskip = jnp.repeat(row(prm["d_skip"]), SSM_HEAD_DIM, axis=1)
    w_out = prm["w_out"]

    x1, q, k, v, z, xbc, dt = _ffn_inproj(
        x.reshape(n, d), row(prm["ffn1_norm"]), w16(prm["ffn1_w_gate"]), w16(prm["ffn1_w_up"]),
        w16(prm["ffn1_w_down"]), row(prm["mix_norm"]), win, wdt, qg, kg, dtb,
        tm=tm, n_ff_chunks=n_ff_chunks)

    r3 = lambda t: t.reshape(b, s, t.shape[-1])
    oa = _diff_attn(r3(q), r3(k), r3(v), row(prm["lambda_q1"]), row(prm["lambda_k1"]),
                    row(prm["lambda_q2"]), row(prm["lambda_k2"]), row(prm["attn_out_norm"]),
                    tq=tq, lam_init=lam_init)
    ys = _ssd(r3(xbc), r3(z), r3(dt), prm["conv_w"].astype(F32), row(prm["conv_b"]), alog, dskip,
              row(prm["ssm_out_norm"]))

    out = _out_ffn_ple(
        x1, oa.reshape(n, ATTN_WIDTH), ys.reshape(n, SSM_WIDTH), p_i.reshape(n, p_i.shape[-1]),
        w16(w_out[:ATTN_WIDTH]), w16(w_out[ATTN_WIDTH:]), row(prm["ffn2_norm"]),
        w16(prm["ffn2_w_gate"]), w16(prm["ffn2_w_up"]), w16(prm["ffn2_w_down"]),
        row(prm["ple_gate_norm"]), w16(prm["w_ple_gate"]), w16(prm["w_ple_proj"]),
        row(prm["ple_norm"]), tm=tm, n_ff_chunks=n_ff_chunks)
    return out.reshape(b, s, d)


def kernel(x, p, ffn1_norm, ffn1_w_gate, ffn1_w_up, ffn1_w_down, mix_norm, w_in, q_norm, k_norm, lambda_q1, lambda_k1, lambda_q2, lambda_k2, attn_out_norm, conv_w, conv_b, dt_bias, a_log, d_skip, ssm_out_norm, w_out, ffn2_norm, ffn2_w_gate, ffn2_w_up, ffn2_w_down, ple_gate_norm, w_ple_gate, w_ple_proj, ple_norm):
    stacked = dict(
        ffn1_norm=ffn1_norm, ffn1_w_gate=ffn1_w_gate, ffn1_w_up=ffn1_w_up, ffn1_w_down=ffn1_w_down,
        mix_norm=mix_norm, w_in=w_in, q_norm=q_norm, k_norm=k_norm, lambda_q1=lambda_q1,
        lambda_k1=lambda_k1, lambda_q2=lambda_q2, lambda_k2=lambda_k2, attn_out_norm=attn_out_norm,
        conv_w=conv_w, conv_b=conv_b, dt_bias=dt_bias, a_log=a_log, d_skip=d_skip,
        ssm_out_norm=ssm_out_norm, w_out=w_out, ffn2_norm=ffn2_norm, ffn2_w_gate=ffn2_w_gate,
        ffn2_w_up=ffn2_w_up, ffn2_w_down=ffn2_w_down, ple_gate_norm=ple_gate_norm,
        w_ple_gate=w_ple_gate, w_ple_proj=w_ple_proj, ple_norm=ple_norm)
    for i in range(p.shape[0]):
        x = _layer(i, x, p[i], {name: w[i] for name, w in stacked.items()})
    return x
```

```python
import functools
import math

import numpy as np
import jax
import jax.numpy as jnp
from jax import lax
from jax.experimental import pallas as pl
from jax.experimental.pallas import tpu as pltpu

N_ATTN_HEADS = 4
ATTN_HEAD_DIM = 64
SSM_HEAD_DIM = 64
N_SSM_HEADS = 8
SSM_GROUPS = 2
SSM_STATE = 128
CONV_WIDTH = 4
CHUNK = 256
NORM_EPS = 1e-6

LANES = 128
SUBLANES = 8
VMEM_LIMIT_BYTES = 56 * 1024 * 1024

ATTN_WIDTH = 2 * N_ATTN_HEADS * ATTN_HEAD_DIM
SSM_WIDTH = N_SSM_HEADS * SSM_HEAD_DIM
HEADS_PER_GROUP = N_SSM_HEADS // SSM_GROUPS
GROUP_WIDTH = HEADS_PER_GROUP * SSM_HEAD_DIM
CONV_DIM = SSM_WIDTH + 2 * SSM_GROUPS * SSM_STATE
NEG_BIG = -1e30

BF16 = jnp.bfloat16
F32 = jnp.float32


def _rms(x, g):
    ms = jnp.mean(x * x, axis=-1, keepdims=True)
    return x * lax.rsqrt(ms + NORM_EPS) * g


def _silu(x):
    return x * jax.nn.sigmoid(x)


def _dot(a, b):
    return jnp.dot(a, b, preferred_element_type=F32)


def _dot_nt(a, b):
    return lax.dot_general(a, b, (((1,), (1,)), ((), ())), preferred_element_type=F32)


def _dot_tn(a, b):
    return lax.dot_general(a, b, (((0,), (0,)), ((), ())), preferred_element_type=F32)


def _resident(shape):
    zeros = (0,) * len(shape)
    return pl.BlockSpec(shape, lambda *_: zeros, pipeline_mode=pl.Buffered(1))


def _swiglu_residual(x, norm, wg_ref, wu_ref, wd_ref, n_ff_chunks):
    h = _rms(x, norm).astype(BF16)
    d_ff = wg_ref.shape[1]
    cf = d_ff // n_ff_chunks
    acc = None
    for c in range(n_ff_chunks):
        g = _dot(h, wg_ref[:, c * cf:(c + 1) * cf])
        u = _dot(h, wu_ref[:, c * cf:(c + 1) * cf])
        a = (_silu(g) * u).astype(BF16)
        d = _dot(a, wd_ref[c * cf:(c + 1) * cf, :])
        acc = d if acc is None else acc + d
    return x + 0.5 * acc


def _pair_rms(x, g):
    lane = lax.broadcasted_iota(jnp.int32, x.shape, 1)
    lo = lane < ATTN_HEAD_DIM
    sq = x * x
    s_all = jnp.sum(sq, axis=-1, keepdims=True)
    s_lo = jnp.sum(jnp.where(lo, sq, 0.0), axis=-1, keepdims=True)
    s_hi = s_all - s_lo
    inv = 1.0 / ATTN_HEAD_DIM
    r = jnp.where(lo, lax.rsqrt(s_lo * inv + NORM_EPS), lax.rsqrt(s_hi * inv + NORM_EPS))
    return x * r * g


def _ffn_inproj_kernel(x_ref, n1_ref, wg_ref, wu_ref, wd_ref, nm_ref, win_ref, wdt_ref,
                       qg_ref, kg_ref, dtb_ref,
                       x1_ref, q_ref, k_ref, v_ref, z_ref, xbc_ref, dt_ref, *, n_ff_chunks):
    x1 = _swiglu_residual(x_ref[...], n1_ref[...], wg_ref, wu_ref, wd_ref, n_ff_chunks)
    x1_ref[...] = x1
    h = _rms(x1, nm_ref[...]).astype(BF16)
    aw = ATTN_WIDTH
    for hd in range(N_ATTN_HEADS):
        sl = slice(hd * LANES, (hd + 1) * LANES)
        qh = _dot(h, win_ref[:, hd * LANES:(hd + 1) * LANES])
        kh = _dot(h, win_ref[:, aw + hd * LANES:aw + (hd + 1) * LANES])
        q_ref[:, sl] = _pair_rms(qh, qg_ref[...]).astype(BF16)
        k_ref[:, sl] = _pair_rms(kh, kg_ref[...]).astype(BF16)
    v_ref[...] = _dot(h, win_ref[:, 2 * aw:3 * aw]).astype(BF16)
    z_ref[...] = _dot(h, win_ref[:, 3 * aw:3 * aw + SSM_WIDTH])
    xbc_ref[...] = _dot(h, win_ref[:, 3 * aw + SSM_WIDTH:])
    dt_ref[...] = _dot(h, wdt_ref[...]) + dtb_ref[...]


def _ffn_inproj(x, n1, wg, wu, wd, nm, win, wdt, qg, kg, dtb, *, tm, n_ff_chunks):
    n, d = x.shape
    d_ff = wg.shape[1]
    row = lambda w: pl.BlockSpec((tm, w), lambda i: (i, 0))
    out_shape = (
        jax.ShapeDtypeStruct((n, d), F32),
        jax.ShapeDtypeStruct((n, ATTN_WIDTH), BF16),
        jax.ShapeDtypeStruct((n, ATTN_WIDTH), BF16),
        jax.ShapeDtypeStruct((n, ATTN_WIDTH), BF16),
        jax.ShapeDtypeStruct((n, SSM_WIDTH), F32),
        jax.ShapeDtypeStruct((n, CONV_DIM), F32),
        jax.ShapeDtypeStruct((n, LANES), F32),
    )
    return pl.pallas_call(
        functools.partial(_ffn_inproj_kernel, n_ff_chunks=n_ff_chunks),
        grid=(n // tm,),
        in_specs=[row(d), _resident((1, d)), _resident((d, d_ff)), _resident((d, d_ff)),
                  _resident((d_ff, d)), _resident((1, d)), _resident(win.shape), _resident(wdt.shape),
                  _resident((1, LANES)), _resident((1, LANES)), _resident((1, LANES))],
        out_specs=(row(d), row(ATTN_WIDTH), row(ATTN_WIDTH), row(ATTN_WIDTH), row(SSM_WIDTH),
                   row(CONV_DIM), row(LANES)),
        out_shape=out_shape,
        compiler_params=pltpu.CompilerParams(dimension_semantics=("arbitrary",),
                                             vmem_limit_bytes=VMEM_LIMIT_BYTES),
        name="ffn_inproj",
    )(x, n1, wg, wu, wd, nm, win, wdt, qg, kg, dtb)


def _bf16_terms(x, n):
    terms = []
    for _ in range(n):
        t = float(np.asarray(x, dtype=BF16))
        terms.append(t)
        x -= t
    return terms


LOG2E = math.log2(math.e)
LOG2E_TERMS = _bf16_terms(LOG2E, 3)
POS_SPLIT = 128
ACC_ROWS = LANES + 16


def _diff_attn_kernel(q_ref, k_ref, v_ref, lq1_ref, lk1_ref, lq2_ref, lk2_ref, on_ref, o_ref,
                      kaug_ref, vt_ref, acc_ref, p_ref, *, tq, tk, prep_rows, lam_init):
    hd = pl.program_id(1)
    qi = pl.program_id(2)
    s_len = k_ref.shape[1]
    n_terms = len(LOG2E_TERMS)

    @pl.when(qi == 0)
    def _():
        slope_bits = (127 - (8 // N_ATTN_HEADS) * (hd + 1)) << 23
        slope = lax.bitcast_convert_type(jnp.full((prep_rows, LANES), slope_bits, jnp.int32), F32)
        lane = lax.broadcasted_iota(jnp.int32, (prep_rows, LANES), 1)
        rowi = lax.broadcasted_iota(jnp.int32, (prep_rows, LANES), 0)

        def prep(t, _):
            start = pl.multiple_of(t * prep_rows, prep_rows)
            pos = start + rowi
            lo = pos & (POS_SPLIT - 1)
            hi_part = slope * (pos - lo).astype(F32)
            lo_part = slope * lo.astype(F32)
            cols = jnp.where(lane < n_terms, hi_part, jnp.where(lane < 2 * n_terms, lo_part, 0.0))
            kaug_ref[pl.ds(start, prep_rows), 0:LANES] = k_ref[0, pl.ds(start, prep_rows), :]
            kaug_ref[pl.ds(start, prep_rows), LANES:2 * LANES] = cols.astype(BF16)
            vt = v_ref[0, pl.ds(start, prep_rows), :].astype(F32).T
            vt_ref[0:LANES, pl.ds(start, prep_rows)] = vt.astype(BF16)
            return 0

        lax.fori_loop(0, s_len // prep_rows, prep, 0)
        pad_row = lax.broadcasted_iota(jnp.int32, (ACC_ROWS - LANES, s_len), 0)
        vt_ref[LANES:ACC_ROWS, :] = jnp.where(pad_row == 0, 1.0, 0.0).astype(BF16)

    q = q_ref[0]
    lane = lax.broadcasted_iota(jnp.int32, q.shape, 1)
    consts = jnp.zeros(q.shape, F32)
    for t, term in enumerate(LOG2E_TERMS):
        consts = jnp.where((lane == t) | (lane == n_terms + t), term, consts)
    consts = consts.astype(BF16)
    zero = jnp.zeros_like(q)
    qaug = (jnp.concatenate([jnp.where(lane < ATTN_HEAD_DIM, q, zero), consts], axis=1),
            jnp.concatenate([jnp.where(lane >= ATTN_HEAD_DIM, q, zero), consts], axis=1))

    key = lax.broadcasted_iota(jnp.int32, (tk, tq), 0)
    qry = lax.broadcasted_iota(jnp.int32, (tk, tq), 1)
    acc_ref[...] = jnp.zeros_like(acc_ref)

    def accumulate(j, alphas):
        vb = vt_ref[:, pl.ds(pl.multiple_of(j * tk, tk), tk)]
        for c in range(2):
            acc_ref[c] = alphas[c] * acc_ref[c] + _dot(vb, p_ref[c])

    def scores(j):
        kb = kaug_ref[pl.ds(pl.multiple_of(j * tk, tk), tk), :]
        return [_dot_nt(kb, qaug[c]) for c in range(2)]

    def softmax_block(ss, ms, masked):
        new, alphas = [], []
        for c in range(2):
            s = ss[c]
            if masked:
                s = jnp.where(key <= qry, s, NEG_BIG)
            m_new = jnp.maximum(ms[c], jnp.max(s, axis=0, keepdims=True))
            p_ref[c] = jnp.exp2(s - m_new).astype(BF16)
            alphas.append(jnp.exp2(ms[c] - m_new))
            new.append(m_new)
        return tuple(new), tuple(alphas)

    def body(j, carry):
        ms, alphas = carry
        ss = scores(j)
        accumulate(jnp.where(j == 0, qi, j - 1), alphas)
        return softmax_block(ss, ms, False)

    m0 = jnp.full((1, tq), NEG_BIG, F32)
    carry = softmax_block(scores(qi), (m0, m0), True)
    ms, alphas = lax.fori_loop(0, qi, body, carry)
    accumulate(jnp.where(qi == 0, 0, qi - 1), alphas)

    lam = (jnp.exp(jnp.sum(lq1_ref[...] * lk1_ref[...], axis=-1, keepdims=True))
           - jnp.exp(jnp.sum(lq2_ref[...] * lk2_ref[...], axis=-1, keepdims=True)) + lam_init)
    a1, a2 = acc_ref[0], acc_ref[1]
    o_t = a1[0:LANES] / a1[LANES:LANES + 1] - lam * (a2[0:LANES] / a2[LANES:LANES + 1])
    o_ref[0] = (_rms(o_t.T, on_ref[...]) * (1.0 - lam_init)).astype(BF16)


def _diff_attn(q, k, v, lq1, lk1, lq2, lk2, onorm, *, tq, tk, lam_init):
    b, s, _ = q.shape
    assert tq == tk, "the causal mask assumes square score blocks"
    qspec = pl.BlockSpec((1, tq, LANES), lambda bi, hi, qi: (bi, qi, hi))
    kvspec = pl.BlockSpec((1, s, LANES), lambda bi, hi, qi: (bi, 0, hi))
    small = lambda w: pl.BlockSpec((1, w), lambda bi, hi, qi: (0, 0))
    return pl.pallas_call(
        functools.partial(_diff_attn_kernel, tq=tq, tk=tk, prep_rows=min(s, 512), lam_init=lam_init),
        grid=(b, N_ATTN_HEADS, s // tq),
        in_specs=[qspec, kvspec, kvspec, small(ATTN_HEAD_DIM), small(ATTN_HEAD_DIM),
                  small(ATTN_HEAD_DIM), small(ATTN_HEAD_DIM), small(LANES)],
        out_specs=qspec,
        out_shape=jax.ShapeDtypeStruct((b, s, ATTN_WIDTH), BF16),
        scratch_shapes=[pltpu.VMEM((s, 2 * LANES), BF16),
                        pltpu.VMEM((ACC_ROWS, s), BF16),
                        pltpu.VMEM((2, ACC_ROWS, tq), F32),
                        pltpu.VMEM((2, tk, tq), BF16)],
        compiler_params=pltpu.CompilerParams(
            dimension_semantics=("arbitrary", "arbitrary", "arbitrary"),
            vmem_limit_bytes=VMEM_LIMIT_BYTES),
        name="diff_attn",
    )(q, k, v, lq1, lk1, lq2, lk2, onorm)


def _split3(x):
    hi = x.astype(BF16)
    r = x - hi.astype(F32)
    mid = r.astype(BF16)
    lo = (r - mid.astype(F32)).astype(BF16)
    return hi, mid, lo


def _expand_heads(cols):
    rows = cols.shape[0]
    lane = lax.broadcasted_iota(jnp.int32, (rows, LANES), 1)
    parts = []
    for pr in range(N_SSM_HEADS // 2):
        a = jnp.broadcast_to(cols[:, 2 * pr:2 * pr + 1], (rows, LANES))
        b = jnp.broadcast_to(cols[:, 2 * pr + 1:2 * pr + 2], (rows, LANES))
        parts.append(jnp.where(lane < SSM_HEAD_DIM, a, b))
    return jnp.concatenate(parts, axis=1)


def _ssd_kernel(xbc_ref, z_ref, dt_ref, cw_ref, cb_ref, alog_ref, dskip_ref, on_ref, y_ref,
                xpad_ref, h_ref):
    c = pl.program_id(1)
    L = CHUNK
    halo = SUBLANES

    @pl.when(c == 0)
    def _():
        xpad_ref[0:halo, :] = jnp.zeros((halo, CONV_DIM), F32)
        h_ref[...] = jnp.zeros_like(h_ref)

    cur = xbc_ref[0]
    xpad_ref[halo:halo + L, :] = cur
    conv = cur * cw_ref[CONV_WIDTH - 1:CONV_WIDTH, :] + cb_ref[...]
    for j in range(CONV_WIDTH - 1):
        off = halo - (CONV_WIDTH - 1) + j
        conv = conv + xpad_ref[off:off + L, :] * cw_ref[j:j + 1, :]
    xpad_ref[0:halo, :] = cur[L - halo:, :]
    act = _silu(conv)
    xs = act[:, :SSM_WIDTH]

    dt = jax.nn.softplus(dt_ref[0])
    a = -jnp.exp(alog_ref[...])
    ad = dt * a

    r = lax.broadcasted_iota(jnp.int32, (L, L), 0)
    cc = lax.broadcasted_iota(jnp.int32, (L, L), 1)
    tril = r >= cc
    tri = jnp.where(tril, 1.0, 0.0).astype(BF16)
    hi, mid, lo = _split3(ad)
    a_cs = _dot(tri, hi) + _dot(tri, mid) + _dot(tri, lo)
    a_cs_t = a_cs.T

    dt_l = _expand_heads(dt)
    acs_l = _expand_heads(a_cs)
    last_l = acs_l[L - 1:L, :]
    xd = xs * dt_l
    xw = (xd * jnp.exp(last_l - acs_l)).astype(BF16)
    e_in = jnp.exp(acs_l)
    e_chunk_t = jnp.exp(a_cs_t[:, L - 1:L])
    xd16 = xd.astype(BF16)
    glane = lax.broadcasted_iota(jnp.int32, (L, GROUP_WIDTH), 1)

    ys = []
    for g in range(SSM_GROUPS):
        bg = act[:, SSM_WIDTH + g * SSM_STATE:SSM_WIDTH + (g + 1) * SSM_STATE].astype(BF16)
        cg = act[:, SSM_WIDTH + (SSM_GROUPS + g) * SSM_STATE:
                 SSM_WIDTH + (SSM_GROUPS + g + 1) * SSM_STATE].astype(BF16)
        gs = slice(g * GROUP_WIDTH, (g + 1) * GROUP_WIDTH)
        cb = _dot_nt(cg, bg)
        xd_g = xd16[:, gs]
        y_g = None
        for hl in range(HEADS_PER_GROUP):
            hd = g * HEADS_PER_GROUP + hl
            seg = a_cs[:, hd:hd + 1] - a_cs_t[hd:hd + 1, :]
            m = (cb * jnp.exp(jnp.where(tril, seg, NEG_BIG))).astype(BF16)
            in_head = (glane >= hl * SSM_HEAD_DIM) & (glane < (hl + 1) * SSM_HEAD_DIM)
            d = _dot(m, jnp.where(in_head, xd_g, jnp.zeros_like(xd_g)))
            y_g = d if y_g is None else y_g + d
        h_in = h_ref[g]
        y_off = _dot_nt(cg, h_in.astype(BF16)) * e_in[:, gs]
        ys.append(y_g + y_off)
        decay_rows = jnp.concatenate(
            [jnp.broadcast_to(e_chunk_t[g * HEADS_PER_GROUP + hl:g * HEADS_PER_GROUP + hl + 1, :],
                              (SSM_HEAD_DIM, SSM_STATE)) for hl in range(HEADS_PER_GROUP)], axis=0)
        h_ref[g] = h_in * decay_rows + _dot_tn(xw[:, gs], bg)

    y = jnp.concatenate(ys, axis=1) + dskip_ref[...] * xs
    y = y * _silu(z_ref[0])
    outs = []
    for g in range(SSM_GROUPS):
        gs = slice(g * GROUP_WIDTH, (g + 1) * GROUP_WIDTH)
        outs.append(_rms(y[:, gs], on_ref[:, gs]))
    y_ref[0] = jnp.concatenate(outs, axis=1).astype(BF16)


def _ssd(xbc, z, dt, cw, cb, alog, dskip, onorm):
    b, s, _ = xbc.shape
    blk = lambda w: pl.BlockSpec((1, CHUNK, w), lambda bi, ci: (bi, ci, 0))
    small = lambda r, w: pl.BlockSpec((r, w), lambda bi, ci: (0, 0))
    return pl.pallas_call(
        _ssd_kernel,
        grid=(b, s // CHUNK),
        in_specs=[blk(CONV_DIM), blk(SSM_WIDTH), blk(LANES), small(CONV_WIDTH, CONV_DIM),
                  small(1, CONV_DIM), small(1, LANES), small(1, SSM_WIDTH), small(1, SSM_WIDTH)],
        out_specs=blk(SSM_WIDTH),
        out_shape=jax.ShapeDtypeStruct((b, s, SSM_WIDTH), BF16),
        scratch_shapes=[pltpu.VMEM((SUBLANES + CHUNK, CONV_DIM), F32),
                        pltpu.VMEM((SSM_GROUPS, GROUP_WIDTH, SSM_STATE), F32)],
        compiler_params=pltpu.CompilerParams(dimension_semantics=("arbitrary", "arbitrary"),
                                             vmem_limit_bytes=VMEM_LIMIT_BYTES),
        name="ssd",
    )(xbc, z, dt, cw, cb, alog, dskip, onorm)


def _out_ffn_ple_kernel(x1_ref, oa_ref, ys_ref, p_ref, woa_ref, woy_ref, n2_ref, wg_ref, wu_ref,
                        wd_ref, ng_ref, wpg_ref, wpp_ref, npl_ref, out_ref, *, n_ff_chunks):
    x2 = x1_ref[...] + _dot(oa_ref[...], woa_ref[...]) + _dot(ys_ref[...], woy_ref[...])
    x3 = _swiglu_residual(x2, n2_ref[...], wg_ref, wu_ref, wd_ref, n_ff_chunks)
    e = _rms(_dot(p_ref[...].astype(BF16), wpp_ref[...]), npl_ref[...])
    gate = jax.nn.sigmoid(_dot(_rms(x3, ng_ref[...]).astype(BF16), wpg_ref[...]))
    out_ref[...] = x3 + gate * e


def _out_ffn_ple(x1, oa, ys, p, woa, woy, n2, wg, wu, wd, ng, wpg, wpp, npl, *, tm, n_ff_chunks):
    n, d = x1.shape
    d_ff = wg.shape[1]
    row = lambda w: pl.BlockSpec((tm, w), lambda i: (i, 0))
    return pl.pallas_call(
        functools.partial(_out_ffn_ple_kernel, n_ff_chunks=n_ff_chunks),
        grid=(n // tm,),
        in_specs=[row(d), row(ATTN_WIDTH), row(SSM_WIDTH), row(p.shape[1]),
                  _resident(woa.shape), _resident(woy.shape), _resident((1, d)),
                  _resident((d, d_ff)), _resident((d, d_ff)), _resident((d_ff, d)),
                  _resident((1, d)), _resident(wpg.shape), _resident(wpp.shape), _resident((1, d))],
        out_specs=row(d),
        out_shape=jax.ShapeDtypeStruct((n, d), F32),
        compiler_params=pltpu.CompilerParams(dimension_semantics=("arbitrary",),
                                             vmem_limit_bytes=VMEM_LIMIT_BYTES),
        name="out_ffn_ple",
    )(x1, oa, ys, p, woa, woy, n2, wg, wu, wd, ng, wpg, wpp, npl)


def _tiles(n, s, d_ff):
    tm = 256 if n % 256 == 0 else SUBLANES
    tq = 512 if s % 512 == 0 else CHUNK
    tk = tq
    n_ff_chunks = 2 if d_ff % (2 * LANES) == 0 else 1
    return tm, tq, tk, n_ff_chunks


def _layer(i, x, p_i, prm):
    b, s, d = x.shape
    n = b * s
    d_ff = prm["ffn1_w_gate"].shape[-1]
    tm, tq, tk, n_ff_chunks = _tiles(n, s, d_ff)
    lam_init = 0.8 - 0.6 * math.exp(-0.3 * i)
    row = lambda v: v.reshape(1, -1).astype(F32)
    w16 = lambda w: w.astype(BF16)

    w_in = prm["w_in"]
    n_main = 3 * ATTN_WIDTH + SSM_WIDTH + CONV_DIM
    win = w16(w_in[:, :n_main])
    wdt = w16(jnp.pad(w_in[:, n_main:], ((0, 0), (0, LANES - N_SSM_HEADS))))
    dtb = jnp.pad(row(prm["dt_bias"]), ((0, 0), (0, LANES - N_SSM_HEADS)))
    alog = jnp.pad(row(prm["a_log"]), ((0, 0), (0, LANES - N_SSM_HEADS)))
    qg = jnp.tile(row(prm["q_norm"]), (1, 2)) * (ATTN_HEAD_DIM ** -0.5 * LOG2E)
    kg = jnp.tile(row(prm["k_norm"]), (1, 2))
    dskip = jnp.repeat(row(prm["d_skip"]), SSM_HEAD_DIM, axis=1)
    w_out = prm["w_out"]

    x1, q, k, v, z, xbc, dt = _ffn_inproj(
        x.reshape(n, d), row(prm["ffn1_norm"]), w16(prm["ffn1_w_gate"]), w16(prm["ffn1_w_up"]),
        w16(prm["ffn1_w_down"]), row(prm["mix_norm"]), win, wdt, qg, kg, dtb,
        tm=tm, n_ff_chunks=n_ff_chunks)

    r3 = lambda t: t.reshape(b, s, t.shape[-1])
    oa = _diff_attn(r3(q), r3(k), r3(v), row(prm["lambda_q1"]), row(prm["lambda_k1"]),
                    row(prm["lambda_q2"]), row(prm["lambda_k2"]), row(prm["attn_out_norm"]),
                    tq=tq, tk=tk, lam_init=lam_init)
    ys = _ssd(r3(xbc), r3(z), r3(dt), prm["conv_w"].astype(F32), row(prm["conv_b"]), alog, dskip,
              row(prm["ssm_out_norm"]))

    out = _out_ffn_ple(
        x1, oa.reshape(n, ATTN_WIDTH), ys.reshape(n, SSM_WIDTH), p_i.reshape(n, p_i.shape[-1]),
        w16(w_out[:ATTN_WIDTH]), w16(w_out[ATTN_WIDTH:]), row(prm["ffn2_norm"]),
        w16(prm["ffn2_w_gate"]), w16(prm["ffn2_w_up"]), w16(prm["ffn2_w_down"]),
        row(prm["ple_gate_norm"]), w16(prm["w_ple_gate"]), w16(prm["w_ple_proj"]),
        row(prm["ple_norm"]), tm=tm, n_ff_chunks=n_ff_chunks)
    return out.reshape(b, s, d)


def kernel(x, p, ffn1_norm, ffn1_w_gate, ffn1_w_up, ffn1_w_down, mix_norm, w_in, q_norm, k_norm, lambda_q1, lambda_k1, lambda_q2, lambda_k2, attn_out_norm, conv_w, conv_b, dt_bias, a_log, d_skip, ssm_out_norm, w_out, ffn2_norm, ffn2_w_gate, ffn2_w_up, ffn2_w_down, ple_gate_norm, w_ple_gate, w_ple_proj, ple_norm):
    stacked = dict(
        ffn1_norm=ffn1_norm, ffn1_w_gate=ffn1_w_gate, ffn1_w_up=ffn1_w_up, ffn1_w_down=ffn1_w_down,
        mix_norm=mix_norm, w_in=w_in, q_norm=q_norm, k_norm=k_norm, lambda_q1=lambda_q1,
        lambda_k1=lambda_k1, lambda_q2=lambda_q2, lambda_k2=lambda_k2, attn_out_norm=attn_out_norm,
        conv_w=conv_w, conv_b=conv_b, dt_bias=dt_bias, a_log=a_log, d_skip=d_skip,
        ssm_out_norm=ssm_out_norm, w_out=w_out, ffn2_norm=ffn2_norm, ffn2_w_gate=ffn2_w_gate,
        ffn2_w_up=ffn2_w_up, ffn2_w_down=ffn2_w_down, ple_gate_norm=ple_gate_norm,
        w_ple_gate=w_ple_gate, w_ple_proj=w_ple_proj, ple_norm=ple_norm)
    for i in range(p.shape[0]):
        x = _layer(i, x, p[i], {name: w[i] for name, w in stacked.items()})
    return x
```

```python
import functools
import math

import numpy as np
import jax
import jax.numpy as jnp
from jax import lax
from jax.experimental import pallas as pl
from jax.experimental.pallas import tpu as pltpu

N_ATTN_HEADS = 4
ATTN_HEAD_DIM = 64
SSM_HEAD_DIM = 64
N_SSM_HEADS = 8
SSM_GROUPS = 2
SSM_STATE = 128
CONV_WIDTH = 4
CHUNK = 256
NORM_EPS = 1e-6

LANES = 128
SUBLANES = 8
MXU_WIDTH = 256
VMEM_LIMIT_BYTES = 56 * 1024 * 1024

ATTN_WIDTH = 2 * N_ATTN_HEADS * ATTN_HEAD_DIM
SSM_WIDTH = N_SSM_HEADS * SSM_HEAD_DIM
HEADS_PER_GROUP = N_SSM_HEADS // SSM_GROUPS
GROUP_WIDTH = HEADS_PER_GROUP * SSM_HEAD_DIM
CONV_DIM = SSM_WIDTH + 2 * SSM_GROUPS * SSM_STATE
NEG_BIG = -1e30

BF16 = jnp.bfloat16
F32 = jnp.float32


def _rms(x, g):
    ms = jnp.mean(x * x, axis=-1, keepdims=True)
    return x * lax.rsqrt(ms + NORM_EPS) * g


def _silu(x):
    return x * jax.nn.sigmoid(x)


def _dot(a, b):
    return jnp.dot(a, b, preferred_element_type=F32)


def _dot_nt(a, b):
    return lax.dot_general(a, b, (((1,), (1,)), ((), ())), preferred_element_type=F32)


def _dot_tn(a, b):
    return lax.dot_general(a, b, (((0,), (0,)), ((), ())), preferred_element_type=F32)


def _resident(shape):
    zeros = (0,) * len(shape)
    return pl.BlockSpec(shape, lambda *_: zeros, pipeline_mode=pl.Buffered(1))


def _swiglu_residual(x, norm, wg_ref, wu_ref, wd_ref, ff_chunk):
    h = _rms(x, norm).astype(BF16)
    d_ff = wg_ref.shape[1]
    acc = None
    for lo in range(0, d_ff, ff_chunk):
        hi = min(lo + ff_chunk, d_ff)
        g = _dot(h, wg_ref[:, lo:hi])
        u = _dot(h, wu_ref[:, lo:hi])
        a = (_silu(g) * u).astype(BF16)
        d = _dot(a, wd_ref[lo:hi, :])
        acc = d if acc is None else acc + d
    return x + 0.5 * acc


def _pair_rms(x, g):
    lane = lax.broadcasted_iota(jnp.int32, x.shape, 1)
    lo = lane < ATTN_HEAD_DIM
    sq = x * x
    s_all = jnp.sum(sq, axis=-1, keepdims=True)
    s_lo = jnp.sum(jnp.where(lo, sq, 0.0), axis=-1, keepdims=True)
    s_hi = s_all - s_lo
    inv = 1.0 / ATTN_HEAD_DIM
    r = jnp.where(lo, lax.rsqrt(s_lo * inv + NORM_EPS), lax.rsqrt(s_hi * inv + NORM_EPS))
    return x * r * g


def _ffn_inproj_kernel(x_ref, n1_ref, wg_ref, wu_ref, wd_ref, nm_ref, win_ref, wdt_ref,
                       qg_ref, kg_ref, dtb_ref,
                       x1_ref, q_ref, k_ref, v_ref, z_ref, xbc_ref, dt_ref, *, ff_chunk):
    x1 = _swiglu_residual(x_ref[...], n1_ref[...], wg_ref, wu_ref, wd_ref, ff_chunk)
    x1_ref[...] = x1
    h = _rms(x1, nm_ref[...]).astype(BF16)
    aw = ATTN_WIDTH
    q = _dot(h, win_ref[:, 0:aw])
    k = _dot(h, win_ref[:, aw:2 * aw])
    for hd in range(N_ATTN_HEADS):
        sl = slice(hd * LANES, (hd + 1) * LANES)
        q_ref[:, sl] = _pair_rms(q[:, sl], qg_ref[...]).astype(BF16)
        k_ref[:, sl] = _pair_rms(k[:, sl], kg_ref[...]).astype(BF16)
    v_ref[...] = _dot(h, win_ref[:, 2 * aw:3 * aw]).astype(BF16)
    z_ref[...] = _dot(h, win_ref[:, 3 * aw:3 * aw + SSM_WIDTH])
    xbc_ref[...] = _dot(h, win_ref[:, 3 * aw + SSM_WIDTH:])
    dt_ref[...] = _dot(h, wdt_ref[...]) + dtb_ref[...]


def _ffn_inproj(x, n1, wg, wu, wd, nm, win, wdt, qg, kg, dtb, *, tm, ff_chunk):
    n, d = x.shape
    d_ff = wg.shape[1]
    row = lambda w: pl.BlockSpec((tm, w), lambda i: (i, 0))
    out_shape = (
        jax.ShapeDtypeStruct((n, d), F32),
        jax.ShapeDtypeStruct((n, ATTN_WIDTH), BF16),
        jax.ShapeDtypeStruct((n, ATTN_WIDTH), BF16),
        jax.ShapeDtypeStruct((n, ATTN_WIDTH), BF16),
        jax.ShapeDtypeStruct((n, SSM_WIDTH), F32),
        jax.ShapeDtypeStruct((n, CONV_DIM), F32),
        jax.ShapeDtypeStruct((n, LANES), F32),
    )
    return pl.pallas_call(
        functools.partial(_ffn_inproj_kernel, ff_chunk=ff_chunk),
        grid=(n // tm,),
        in_specs=[row(d), _resident((1, d)), _resident((d, d_ff)), _resident((d, d_ff)),
                  _resident((d_ff, d)), _resident((1, d)), _resident(win.shape), _resident(wdt.shape),
                  _resident((1, LANES)), _resident((1, LANES)), _resident((1, LANES))],
        out_specs=(row(d), row(ATTN_WIDTH), row(ATTN_WIDTH), row(ATTN_WIDTH), row(SSM_WIDTH),
                   row(CONV_DIM), row(LANES)),
        out_shape=out_shape,
        compiler_params=pltpu.CompilerParams(dimension_semantics=("arbitrary",),
                                             vmem_limit_bytes=VMEM_LIMIT_BYTES),
        name="ffn_inproj",
    )(x, n1, wg, wu, wd, nm, win, wdt, qg, kg, dtb)


def _bf16_terms(x, n):
    terms = []
    for _ in range(n):
        t = float(np.asarray(x, dtype=BF16))
        terms.append(t)
        x -= t
    return terms


LOG2E = math.log2(math.e)
LOG2E_TERMS = _bf16_terms(LOG2E, 3)
POS_SPLIT = 128
ACC_ROWS = LANES + 16


def _diff_attn_kernel(q_ref, k_ref, v_ref, lq1_ref, lk1_ref, lq2_ref, lk2_ref, on_ref, o_ref,
                      kaug_ref, vt_ref, acc_ref, p_ref, *, tq, tk, prep_rows, lam_init):
    hd = pl.program_id(1)
    qi = pl.program_id(2)
    s_len = k_ref.shape[1]
    n_terms = len(LOG2E_TERMS)

    @pl.when(qi == 0)
    def _():
        slope_bits = (127 - (8 // N_ATTN_HEADS) * (hd + 1)) << 23
        slope = lax.bitcast_convert_type(jnp.full((prep_rows, LANES), slope_bits, jnp.int32), F32)
        lane = lax.broadcasted_iota(jnp.int32, (prep_rows, LANES), 1)
        rowi = lax.broadcasted_iota(jnp.int32, (prep_rows, LANES), 0)

        def prep(t, _):
            start = pl.multiple_of(t * prep_rows, prep_rows)
            pos = start + rowi
            lo = pos & (POS_SPLIT - 1)
            hi_part = slope * (pos - lo).astype(F32)
            lo_part = slope * lo.astype(F32)
            cols = jnp.where(lane < n_terms, hi_part, jnp.where(lane < 2 * n_terms, lo_part, 0.0))
            kaug_ref[pl.ds(start, prep_rows), 0:LANES] = k_ref[0, pl.ds(start, prep_rows), :]
            kaug_ref[pl.ds(start, prep_rows), LANES:2 * LANES] = cols.astype(BF16)
            vt = v_ref[0, pl.ds(start, prep_rows), :].astype(F32).T
            vt_ref[0:LANES, pl.ds(start, prep_rows)] = vt.astype(BF16)
            return 0

        lax.fori_loop(0, s_len // prep_rows, prep, 0)
        pad_row = lax.broadcasted_iota(jnp.int32, (ACC_ROWS - LANES, s_len), 0)
        vt_ref[LANES:ACC_ROWS, :] = jnp.where(pad_row == 0, 1.0, 0.0).astype(BF16)

    q = q_ref[0]
    lane = lax.broadcasted_iota(jnp.int32, q.shape, 1)
    consts = jnp.zeros(q.shape, F32)
    for t, term in enumerate(LOG2E_TERMS):
        consts = jnp.where((lane == t) | (lane == n_terms + t), term, consts)
    consts = consts.astype(BF16)
    zero = jnp.zeros_like(q)
    qaug = (jnp.concatenate([jnp.where(lane < ATTN_HEAD_DIM, q, zero), consts], axis=1),
            jnp.concatenate([jnp.where(lane >= ATTN_HEAD_DIM, q, zero), consts], axis=1))

    key = lax.broadcasted_iota(jnp.int32, (tk, tq), 0)
    qry = lax.broadcasted_iota(jnp.int32, (tk, tq), 1)
    acc_ref[...] = jnp.zeros_like(acc_ref)

    def accumulate(j, alphas):
        vb = vt_ref[:, pl.ds(pl.multiple_of(j * tk, tk), tk)]
        for c in range(2):
            acc_ref[c] = alphas[c] * acc_ref[c] + _dot(vb, p_ref[c])

    def scores(j):
        kb = kaug_ref[pl.ds(pl.multiple_of(j * tk, tk), tk), :]
        return [_dot_nt(kb, qaug[c]) for c in range(2)]

    def softmax_block(ss, ms, masked):
        new, alphas = [], []
        for c in range(2):
            s = ss[c]
            if masked:
                s = jnp.where(key <= qry, s, NEG_BIG)
            m_new = jnp.maximum(ms[c], jnp.max(s, axis=0, keepdims=True))
            p_ref[c] = jnp.exp2(s - m_new).astype(BF16)
            alphas.append(jnp.exp2(ms[c] - m_new))
            new.append(m_new)
        return tuple(new), tuple(alphas)

    def body(j, carry):
        ms, alphas = carry
        ss = scores(j)
        accumulate(jnp.where(j == 0, qi, j - 1), alphas)
        return softmax_block(ss, ms, False)

    m0 = jnp.full((1, tq), NEG_BIG, F32)
    carry = softmax_block(scores(qi), (m0, m0), True)
    ms, alphas = lax.fori_loop(0, qi, body, carry)
    accumulate(jnp.where(qi == 0, 0, qi - 1), alphas)

    lam = (jnp.exp(jnp.sum(lq1_ref[...] * lk1_ref[...], axis=-1, keepdims=True))
           - jnp.exp(jnp.sum(lq2_ref[...] * lk2_ref[...], axis=-1, keepdims=True)) + lam_init)
    a1, a2 = acc_ref[0], acc_ref[1]
    o_t = a1[0:LANES] / a1[LANES:LANES + 1] - lam * (a2[0:LANES] / a2[LANES:LANES + 1])
    o_ref[0] = (_rms(o_t.T, on_ref[...]) * (1.0 - lam_init)).astype(BF16)


def _diff_attn(q, k, v, lq1, lk1, lq2, lk2, onorm, *, tq, tk, lam_init):
    b, s, _ = q.shape
    assert tq == tk, "the causal mask assumes square score blocks"
    qspec = pl.BlockSpec((1, tq, LANES), lambda bi, hi, qi: (bi, qi, hi))
    kvspec = pl.BlockSpec((1, s, LANES), lambda bi, hi, qi: (bi, 0, hi))
    small = lambda w: pl.BlockSpec((1, w), lambda bi, hi, qi: (0, 0))
    return pl.pallas_call(
        functools.partial(_diff_attn_kernel, tq=tq, tk=tk, prep_rows=min(s, 512), lam_init=lam_init),
        grid=(b, N_ATTN_HEADS, s // tq),
        in_specs=[qspec, kvspec, kvspec, small(ATTN_HEAD_DIM), small(ATTN_HEAD_DIM),
                  small(ATTN_HEAD_DIM), small(ATTN_HEAD_DIM), small(LANES)],
        out_specs=qspec,
        out_shape=jax.ShapeDtypeStruct((b, s, ATTN_WIDTH), BF16),
        scratch_shapes=[pltpu.VMEM((s, 2 * LANES), BF16),
                        pltpu.VMEM((ACC_ROWS, s), BF16),
                        pltpu.VMEM((2, ACC_ROWS, tq), F32),
                        pltpu.VMEM((2, tk, tq), BF16)],
        compiler_params=pltpu.CompilerParams(
            dimension_semantics=("arbitrary", "arbitrary", "arbitrary"),
            vmem_limit_bytes=VMEM_LIMIT_BYTES),
        name="diff_attn",
    )(q, k, v, lq1, lk1, lq2, lk2, onorm)


def _split3(x):
    hi = x.astype(BF16)
    r = x - hi.astype(F32)
    mid = r.astype(BF16)
    lo = (r - mid.astype(F32)).astype(BF16)
    return hi, mid, lo


def _expand_heads(cols):
    rows = cols.shape[0]
    lane = lax.broadcasted_iota(jnp.int32, (rows, LANES), 1)
    parts = []
    for pr in range(N_SSM_HEADS // 2):
        a = jnp.broadcast_to(cols[:, 2 * pr:2 * pr + 1], (rows, LANES))
        b = jnp.broadcast_to(cols[:, 2 * pr + 1:2 * pr + 2], (rows, LANES))
        parts.append(jnp.where(lane < SSM_HEAD_DIM, a, b))
    return jnp.concatenate(parts, axis=1)


def _ssd_kernel(xbc_ref, z_ref, dt_ref, cw_ref, cb_ref, alog_ref, dskip_ref, on_ref, y_ref,
                xpad_ref, h_ref):
    c = pl.program_id(1)
    L = CHUNK
    halo = SUBLANES

    @pl.when(c == 0)
    def _():
        xpad_ref[0:halo, :] = jnp.zeros((halo, CONV_DIM), F32)
        h_ref[...] = jnp.zeros_like(h_ref)

    cur = xbc_ref[0]
    xpad_ref[halo:halo + L, :] = cur
    conv = cur * cw_ref[CONV_WIDTH - 1:CONV_WIDTH, :] + cb_ref[...]
    for j in range(CONV_WIDTH - 1):
        off = halo - (CONV_WIDTH - 1) + j
        conv = conv + xpad_ref[off:off + L, :] * cw_ref[j:j + 1, :]
    xpad_ref[0:halo, :] = cur[L - halo:, :]
    act = _silu(conv)
    xs = act[:, :SSM_WIDTH]

    dt = jax.nn.softplus(dt_ref[0])
    a = -jnp.exp(alog_ref[...])
    ad = dt * a

    r = lax.broadcasted_iota(jnp.int32, (L, L), 0)
    cc = lax.broadcasted_iota(jnp.int32, (L, L), 1)
    tril = r >= cc
    tri = jnp.where(tril, 1.0, 0.0).astype(BF16)
    hi, mid, lo = _split3(ad)
    a_cs = _dot(tri, hi) + _dot(tri, mid) + _dot(tri, lo)
    a_cs_t = a_cs.T

    dt_l = _expand_heads(dt)
    acs_l = _expand_heads(a_cs)
    last_l = acs_l[L - 1:L, :]
    xd = xs * dt_l
    xw = (xd * jnp.exp(last_l - acs_l)).astype(BF16)
    e_in = jnp.exp(acs_l)
    e_chunk_t = jnp.exp(a_cs_t[:, L - 1:L])
    xd16 = xd.astype(BF16)
    glane = lax.broadcasted_iota(jnp.int32, (L, GROUP_WIDTH), 1)

    ys = []
    for g in range(SSM_GROUPS):
        bg = act[:, SSM_WIDTH + g * SSM_STATE:SSM_WIDTH + (g + 1) * SSM_STATE].astype(BF16)
        cg = act[:, SSM_WIDTH + (SSM_GROUPS + g) * SSM_STATE:
                 SSM_WIDTH + (SSM_GROUPS + g + 1) * SSM_STATE].astype(BF16)
        gs = slice(g * GROUP_WIDTH, (g + 1) * GROUP_WIDTH)
        cb = _dot_nt(cg, bg)
        xd_g = xd16[:, gs]
        y_g = None
        for hl in range(HEADS_PER_GROUP):
            hd = g * HEADS_PER_GROUP + hl
            seg = a_cs[:, hd:hd + 1] - a_cs_t[hd:hd + 1, :]
            m = (cb * jnp.exp(jnp.where(tril, seg, NEG_BIG))).astype(BF16)
            in_head = (glane >= hl * SSM_HEAD_DIM) & (glane < (hl + 1) * SSM_HEAD_DIM)
            d = _dot(m, jnp.where(in_head, xd_g, jnp.zeros_like(xd_g)))
            y_g = d if y_g is None else y_g + d
        h_in = h_ref[g]
        y_off = _dot_nt(cg, h_in.astype(BF16)) * e_in[:, gs]
        ys.append(y_g + y_off)
        decay_rows = jnp.concatenate(
            [jnp.broadcast_to(e_chunk_t[g * HEADS_PER_GROUP + hl:g * HEADS_PER_GROUP + hl + 1, :],
                              (SSM_HEAD_DIM, SSM_STATE)) for hl in range(HEADS_PER_GROUP)], axis=0)
        h_ref[g] = h_in * decay_rows + _dot_tn(xw[:, gs], bg)

    y = jnp.concatenate(ys, axis=1) + dskip_ref[...] * xs
    y = y * _silu(z_ref[0])
    outs = []
    for g in range(SSM_GROUPS):
        gs = slice(g * GROUP_WIDTH, (g + 1) * GROUP_WIDTH)
        outs.append(_rms(y[:, gs], on_ref[:, gs]))
    y_ref[0] = jnp.concatenate(outs, axis=1).astype(BF16)


def _ssd(xbc, z, dt, cw, cb, alog, dskip, onorm):
    b, s, _ = xbc.shape
    blk = lambda w: pl.BlockSpec((1, CHUNK, w), lambda bi, ci: (bi, ci, 0))
    small = lambda r, w: pl.BlockSpec((r, w), lambda bi, ci: (0, 0))
    return pl.pallas_call(
        _ssd_kernel,
        grid=(b, s // CHUNK),
        in_specs=[blk(CONV_DIM), blk(SSM_WIDTH), blk(LANES), small(CONV_WIDTH, CONV_DIM),
                  small(1, CONV_DIM), small(1, LANES), small(1, SSM_WIDTH), small(1, SSM_WIDTH)],
        out_specs=blk(SSM_WIDTH),
        out_shape=jax.ShapeDtypeStruct((b, s, SSM_WIDTH), BF16),
        scratch_shapes=[pltpu.VMEM((SUBLANES + CHUNK, CONV_DIM), F32),
                        pltpu.VMEM((SSM_GROUPS, GROUP_WIDTH, SSM_STATE), F32)],
        compiler_params=pltpu.CompilerParams(dimension_semantics=("arbitrary", "arbitrary"),
                                             vmem_limit_bytes=VMEM_LIMIT_BYTES),
        name="ssd",
    )(xbc, z, dt, cw, cb, alog, dskip, onorm)


def _out_ffn_ple_kernel(x1_ref, oa_ref, ys_ref, p_ref, woa_ref, woy_ref, n2_ref, wg_ref, wu_ref,
                        wd_ref, ng_ref, wpg_ref, wpp_ref, npl_ref, out_ref, *, ff_chunk):
    x2 = x1_ref[...] + _dot(oa_ref[...], woa_ref[...]) + _dot(ys_ref[...], woy_ref[...])
    x3 = _swiglu_residual(x2, n2_ref[...], wg_ref, wu_ref, wd_ref, ff_chunk)
    e = _rms(_dot(p_ref[...].astype(BF16), wpp_ref[...]), npl_ref[...])
    gate = jax.nn.sigmoid(_dot(_rms(x3, ng_ref[...]).astype(BF16), wpg_ref[...]))
    out_ref[...] = x3 + gate * e


def _out_ffn_ple(x1, oa, ys, p, woa, woy, n2, wg, wu, wd, ng, wpg, wpp, npl, *, tm, ff_chunk):
    n, d = x1.shape
    d_ff = wg.shape[1]
    row = lambda w: pl.BlockSpec((tm, w), lambda i: (i, 0))
    return pl.pallas_call(
        functools.partial(_out_ffn_ple_kernel, ff_chunk=ff_chunk),
        grid=(n // tm,),
        in_specs=[row(d), row(ATTN_WIDTH), row(SSM_WIDTH), row(p.shape[1]),
                  _resident(woa.shape), _resident(woy.shape), _resident((1, d)),
                  _resident((d, d_ff)), _resident((d, d_ff)), _resident((d_ff, d)),
                  _resident((1, d)), _resident(wpg.shape), _resident(wpp.shape), _resident((1, d))],
        out_specs=row(d),
        out_shape=jax.ShapeDtypeStruct((n, d), F32),
        compiler_params=pltpu.CompilerParams(dimension_semantics=("arbitrary",),
                                             vmem_limit_bytes=VMEM_LIMIT_BYTES),
        name="out_ffn_ple",
    )(x1, oa, ys, p, woa, woy, n2, wg, wu, wd, ng, wpg, wpp, npl)


def _tiles(n, s, d_ff):
    tm = 512 if n % 512 == 0 else SUBLANES
    tq = 512 if s % 512 == 0 else CHUNK
    tk = tq
    ff_chunk = 6 * MXU_WIDTH
    return tm, tq, tk, ff_chunk


def _layer(i, x, p_i, prm):
    b, s, d = x.shape
    n = b * s
    d_ff = prm["ffn1_w_gate"].shape[-1]
    tm, tq, tk, ff_chunk = _tiles(n, s, d_ff)
    lam_init = 0.8 - 0.6 * math.exp(-0.3 * i)
    row = lambda v: v.reshape(1, -1).astype(F32)
    w16 = lambda w: w.astype(BF16)

    w_in = prm["w_in"]
    n_main = 3 * ATTN_WIDTH + SSM_WIDTH + CONV_DIM
    win = w16(w_in[:, :n_main])
    wdt = w16(jnp.pad(w_in[:, n_main:], ((0, 0), (0, LANES - N_SSM_HEADS))))
    dtb = jnp.pad(row(prm["dt_bias"]), ((0, 0), (0, LANES - N_SSM_HEADS)))
    alog = jnp.pad(row(prm["a_log"]), ((0, 0), (0, LANES - N_SSM_HEADS)))
    qg = jnp.tile(row(prm["q_norm"]), (1, 2)) * (ATTN_HEAD_DIM ** -0.5 * LOG2E)
    kg = jnp.tile(row(prm["k_norm"]), (1, 2))
    dskip = jnp.repeat(row(prm["d_skip"]), SSM_HEAD_DIM, axis=1)
    w_out = prm["w_out"]

    x1, q, k, v, z, xbc, dt = _ffn_inproj(
        x.reshape(n, d), row(prm["ffn1_norm"]), w16(prm["ffn1_w_gate"]), w16(prm["ffn1_w_up"]),
        w16(prm["ffn1_w_down"]), row(prm["mix_norm"]), win, wdt, qg, kg, dtb,
        tm=tm, ff_chunk=ff_chunk)

    r3 = lambda t: t.reshape(b, s, t.shape[-1])
    oa = _diff_attn(r3(q), r3(k), r3(v), row(prm["lambda_q1"]), row(prm["lambda_k1"]),
                    row(prm["lambda_q2"]), row(prm["lambda_k2"]), row(prm["attn_out_norm"]),
                    tq=tq, tk=tk, lam_init=lam_init)
    ys = _ssd(r3(xbc), r3(z), r3(dt), prm["conv_w"].astype(F32), row(prm["conv_b"]), alog, dskip,
              row(prm["ssm_out_norm"]))

    out = _out_ffn_ple(
        x1, oa.reshape(n, ATTN_WIDTH), ys.reshape(n, SSM_WIDTH), p_i.reshape(n, p_i.shape[-1]),
        w16(w_out[:ATTN_WIDTH]), w16(w_out[ATTN_WIDTH:]), row(prm["ffn2_norm"]),
        w16(prm["ffn2_w_gate"]), w16(prm["ffn2_w_up"]), w16(prm["ffn2_w_down"]),
        row(prm["ple_gate_norm"]), w16(prm["w_ple_gate"]), w16(prm["w_ple_proj"]),
        row(prm["ple_norm"]), tm=tm, ff_chunk=ff_chunk)
    return out.reshape(b, s, d)


def kernel(x, p, ffn1_norm, ffn1_w_gate, ffn1_w_up, ffn1_w_down, mix_norm, w_in, q_norm, k_norm, lambda_q1, lambda_k1, lambda_q2, lambda_k2, attn_out_norm, conv_w, conv_b, dt_bias, a_log, d_skip, ssm_out_norm, w_out, ffn2_norm, ffn2_w_gate, ffn2_w_up, ffn2_w_down, ple_gate_norm, w_ple_gate, w_ple_proj, ple_norm):
    stacked = dict(
        ffn1_norm=ffn1_norm, ffn1_w_gate=ffn1_w_gate, ffn1_w_up=ffn1_w_up, ffn1_w_down=ffn1_w_down,
        mix_norm=mix_norm, w_in=w_in, q_norm=q_norm, k_norm=k_norm, lambda_q1=lambda_q1,
        lambda_k1=lambda_k1, lambda_q2=lambda_q2, lambda_k2=lambda_k2, attn_out_norm=attn_out_norm,
        conv_w=conv_w, conv_b=conv_b, dt_bias=dt_bias, a_log=a_log, d_skip=d_skip,
        ssm_out_norm=ssm_out_norm, w_out=w_out, ffn2_norm=ffn2_norm, ffn2_w_gate=ffn2_w_gate,
        ffn2_w_up=ffn2_w_up, ffn2_w_down=ffn2_w_down, ple_gate_norm=ple_gate_norm,
        w_ple_gate=w_ple_gate, w_ple_proj=w_ple_proj, ple_norm=ple_norm)
    for i in range(p.shape[0]):
        x = _layer(i, x, p[i], {name: w[i] for name, w in stacked.items()})
    return x
```

```python
import functools
import math

import numpy as np
import jax
import jax.numpy as jnp
from jax import lax
from jax.experimental import pallas as pl
from jax.experimental.pallas import tpu as pltpu

N_ATTN_HEADS = 4
ATTN_HEAD_DIM = 64
SSM_HEAD_DIM = 64
N_SSM_HEADS = 8
SSM_GROUPS = 2
SSM_STATE = 128
CONV_WIDTH = 4
CHUNK = 256
NORM_EPS = 1e-6

LANES = 128
SUBLANES = 8
MXU_WIDTH = 256
VMEM_LIMIT_BYTES = 56 * 1024 * 1024

ATTN_WIDTH = 2 * N_ATTN_HEADS * ATTN_HEAD_DIM
SSM_WIDTH = N_SSM_HEADS * SSM_HEAD_DIM
HEADS_PER_GROUP = N_SSM_HEADS // SSM_GROUPS
GROUP_WIDTH = HEADS_PER_GROUP * SSM_HEAD_DIM
CONV_DIM = SSM_WIDTH + 2 * SSM_GROUPS * SSM_STATE
NEG_BIG = -1e30

BF16 = jnp.bfloat16
F32 = jnp.float32


def _rms(x, g):
    ms = jnp.mean(x * x, axis=-1, keepdims=True)
    return x * lax.rsqrt(ms + NORM_EPS) * g


def _silu(x):
    return x * jax.nn.sigmoid(x)


def _dot(a, b):
    return jnp.dot(a, b, preferred_element_type=F32)


def _dot_nt(a, b):
    return lax.dot_general(a, b, (((1,), (1,)), ((), ())), preferred_element_type=F32)


def _dot_tn(a, b):
    return lax.dot_general(a, b, (((0,), (0,)), ((), ())), preferred_element_type=F32)


def _resident(shape):
    zeros = (0,) * len(shape)
    return pl.BlockSpec(shape, lambda *_: zeros, pipeline_mode=pl.Buffered(1))


def _swiglu_residual(x, norm, wg_ref, wu_ref, wd_ref, ff_chunk):
    h = _rms(x, norm).astype(BF16)
    d_ff = wg_ref.shape[1]
    acc = None
    for lo in range(0, d_ff, ff_chunk):
        hi = min(lo + ff_chunk, d_ff)
        g = _dot(h, wg_ref[:, lo:hi])
        u = _dot(h, wu_ref[:, lo:hi])
        a = (_silu(g) * u).astype(BF16)
        d = _dot(a, wd_ref[lo:hi, :])
        acc = d if acc is None else acc + d
    return x + 0.5 * acc


def _pair_rms(x, g):
    lane = lax.broadcasted_iota(jnp.int32, x.shape, 1)
    lo = lane < ATTN_HEAD_DIM
    sq = x * x
    s_all = jnp.sum(sq, axis=-1, keepdims=True)
    s_lo = jnp.sum(jnp.where(lo, sq, 0.0), axis=-1, keepdims=True)
    s_hi = s_all - s_lo
    inv = 1.0 / ATTN_HEAD_DIM
    r = jnp.where(lo, lax.rsqrt(s_lo * inv + NORM_EPS), lax.rsqrt(s_hi * inv + NORM_EPS))
    return x * r * g


def _ffn_inproj_kernel(x_ref, n1_ref, wg_ref, wu_ref, wd_ref, nm_ref, win_ref, wdt_ref,
                       qg_ref, kg_ref, dtb_ref,
                       x1_ref, q_ref, k_ref, v_ref, z_ref, xbc_ref, dt_ref, *, ff_chunk):
    x1 = _swiglu_residual(x_ref[...], n1_ref[...], wg_ref, wu_ref, wd_ref, ff_chunk)
    x1_ref[...] = x1
    h = _rms(x1, nm_ref[...]).astype(BF16)
    aw = ATTN_WIDTH
    q = _dot(h, win_ref[:, 0:aw])
    k = _dot(h, win_ref[:, aw:2 * aw])
    for hd in range(N_ATTN_HEADS):
        sl = slice(hd * LANES, (hd + 1) * LANES)
        q_ref[:, sl] = _pair_rms(q[:, sl], qg_ref[...]).astype(BF16)
        k_ref[:, sl] = _pair_rms(k[:, sl], kg_ref[...]).astype(BF16)
    v_ref[...] = _dot(h, win_ref[:, 2 * aw:3 * aw]).astype(BF16)
    z_ref[...] = _dot(h, win_ref[:, 3 * aw:3 * aw + SSM_WIDTH])
    xbc_ref[...] = _dot(h, win_ref[:, 3 * aw + SSM_WIDTH:])
    dt_ref[...] = _dot(h, wdt_ref[...]) + dtb_ref[...]


def _ffn_inproj(x, n1, wg, wu, wd, nm, win, wdt, qg, kg, dtb, *, tm, ff_chunk):
    n, d = x.shape
    d_ff = wg.shape[1]
    row = lambda w: pl.BlockSpec((tm, w), lambda i: (i, 0))
    out_shape = (
        jax.ShapeDtypeStruct((n, d), F32),
        jax.ShapeDtypeStruct((n, ATTN_WIDTH), BF16),
        jax.ShapeDtypeStruct((n, ATTN_WIDTH), BF16),
        jax.ShapeDtypeStruct((n, ATTN_WIDTH), BF16),
        jax.ShapeDtypeStruct((n, SSM_WIDTH), F32),
        jax.ShapeDtypeStruct((n, CONV_DIM), F32),
        jax.ShapeDtypeStruct((n, LANES), F32),
    )
    return pl.pallas_call(
        functools.partial(_ffn_inproj_kernel, ff_chunk=ff_chunk),
        grid=(n // tm,),
        in_specs=[row(d), _resident((1, d)), _resident((d, d_ff)), _resident((d, d_ff)),
                  _resident((d_ff, d)), _resident((1, d)), _resident(win.shape), _resident(wdt.shape),
                  _resident((1, LANES)), _resident((1, LANES)), _resident((1, LANES))],
        out_specs=(row(d), row(ATTN_WIDTH), row(ATTN_WIDTH), row(ATTN_WIDTH), row(SSM_WIDTH),
                   row(CONV_DIM), row(LANES)),
        out_shape=out_shape,
        compiler_params=pltpu.CompilerParams(dimension_semantics=("arbitrary",),
                                             vmem_limit_bytes=VMEM_LIMIT_BYTES),
        name="ffn_inproj",
    )(x, n1, wg, wu, wd, nm, win, wdt, qg, kg, dtb)


def _bf16_terms(x, n):
    terms = []
    for _ in range(n):
        t = float(np.asarray(x, dtype=BF16))
        terms.append(t)
        x -= t
    return terms


LOG2E = math.log2(math.e)
LOG2E_TERMS = _bf16_terms(LOG2E, 3)
POS_SPLIT = 128
ACC_ROWS = LANES + 16
KEY_BLOCKS_PER_TRIP = 4


def _diff_attn_kernel(q_ref, k_ref, v_ref, lq1_ref, lk1_ref, lq2_ref, lk2_ref, on_ref, o_ref,
                      kaug_ref, vt_ref, acc_ref, p_ref, *, tq, tk, unroll, prep_rows, lam_init):
    hd = pl.program_id(1)
    qi = pl.program_id(2)
    s_len = k_ref.shape[1]
    n_terms = len(LOG2E_TERMS)

    @pl.when(qi == 0)
    def _():
        slope_bits = (127 - (8 // N_ATTN_HEADS) * (hd + 1)) << 23
        slope = lax.bitcast_convert_type(jnp.full((prep_rows, LANES), slope_bits, jnp.int32), F32)
        lane = lax.broadcasted_iota(jnp.int32, (prep_rows, LANES), 1)
        rowi = lax.broadcasted_iota(jnp.int32, (prep_rows, LANES), 0)

        def prep(t, _):
            start = pl.multiple_of(t * prep_rows, prep_rows)
            pos = start + rowi
            lo = pos & (POS_SPLIT - 1)
            hi_part = slope * (pos - lo).astype(F32)
            lo_part = slope * lo.astype(F32)
            cols = jnp.where(lane < n_terms, hi_part, jnp.where(lane < 2 * n_terms, lo_part, 0.0))
            kaug_ref[pl.ds(start, prep_rows), 0:LANES] = k_ref[0, pl.ds(start, prep_rows), :]
            kaug_ref[pl.ds(start, prep_rows), LANES:2 * LANES] = cols.astype(BF16)
            vt = v_ref[0, pl.ds(start, prep_rows), :].astype(F32).T
            vt_ref[0:LANES, pl.ds(start, prep_rows)] = vt.astype(BF16)
            return 0

        lax.fori_loop(0, s_len // prep_rows, prep, 0)
        pad_row = lax.broadcasted_iota(jnp.int32, (ACC_ROWS - LANES, s_len), 0)
        vt_ref[LANES:ACC_ROWS, :] = jnp.where(pad_row == 0, 1.0, 0.0).astype(BF16)

    q = q_ref[0]
    lane = lax.broadcasted_iota(jnp.int32, q.shape, 1)
    consts = jnp.zeros(q.shape, F32)
    for t, term in enumerate(LOG2E_TERMS):
        consts = jnp.where((lane == t) | (lane == n_terms + t), term, consts)
    consts = consts.astype(BF16)
    zero = jnp.zeros_like(q)
    qaug = (jnp.concatenate([jnp.where(lane < ATTN_HEAD_DIM, q, zero), consts], axis=1),
            jnp.concatenate([jnp.where(lane >= ATTN_HEAD_DIM, q, zero), consts], axis=1))

    key = lax.broadcasted_iota(jnp.int32, (tk, tq), 0)
    qry = lax.broadcasted_iota(jnp.int32, (tk, tq), 1)
    acc_ref[...] = jnp.zeros_like(acc_ref)

    def scores(j):
        kb = kaug_ref[pl.ds(pl.multiple_of(j * tk, tk), tk), :]
        return [_dot_nt(kb, qaug[c]) for c in range(2)]

    def softmax_block(ss, ms, masked):
        new, alphas, ps = [], [], []
        for c in range(2):
            s = ss[c]
            if masked:
                s = jnp.where(key <= qry, s, NEG_BIG)
            m_new = jnp.maximum(ms[c], jnp.max(s, axis=0, keepdims=True))
            ps.append(jnp.exp2(s - m_new).astype(BF16))
            alphas.append(jnp.exp2(ms[c] - m_new))
            new.append(m_new)
        return tuple(new), tuple(alphas), ps

    def accumulate(j, alphas, ps):
        vb = vt_ref[:, pl.ds(pl.multiple_of(j * tk, tk), tk)]
        for c in range(2):
            acc_ref[c] = alphas[c] * acc_ref[c] + _dot(vb, ps[c])

    def blocks(n_blocks):
        def body(t, carry):
            ms, alphas = carry
            ps = [p_ref[0], p_ref[1]]
            for u in range(n_blocks):
                j = t * n_blocks + u
                ss = scores(j)
                accumulate(jnp.where(j == 0, qi, j - 1), alphas, ps)
                ms, alphas, ps = softmax_block(ss, ms, False)
            p_ref[0], p_ref[1] = ps
            return ms, alphas
        return body

    m0 = jnp.full((1, tq), NEG_BIG, F32)
    ms, alphas, ps = softmax_block(scores(qi), (m0, m0), True)
    p_ref[0], p_ref[1] = ps
    n_trips = qi // unroll
    carry = lax.fori_loop(0, n_trips, blocks(unroll), (ms, alphas))
    ms, alphas = lax.fori_loop(n_trips * unroll, qi, blocks(1), carry)
    accumulate(jnp.where(qi == 0, 0, qi - 1), alphas, [p_ref[0], p_ref[1]])

    lam = (jnp.exp(jnp.sum(lq1_ref[...] * lk1_ref[...], axis=-1, keepdims=True))
           - jnp.exp(jnp.sum(lq2_ref[...] * lk2_ref[...], axis=-1, keepdims=True)) + lam_init)
    a1, a2 = acc_ref[0], acc_ref[1]
    o_t = a1[0:LANES] / a1[LANES:LANES + 1] - lam * (a2[0:LANES] / a2[LANES:LANES + 1])
    o_ref[0] = (_rms(o_t.T, on_ref[...]) * (1.0 - lam_init)).astype(BF16)


def _diff_attn(q, k, v, lq1, lk1, lq2, lk2, onorm, *, tq, tk, lam_init):
    b, s, _ = q.shape
    assert tq == tk, "the causal mask assumes square score blocks"
    qspec = pl.BlockSpec((1, tq, LANES), lambda bi, hi, qi: (bi, qi, hi))
    kvspec = pl.BlockSpec((1, s, LANES), lambda bi, hi, qi: (bi, 0, hi))
    small = lambda w: pl.BlockSpec((1, w), lambda bi, hi, qi: (0, 0))
    return pl.pallas_call(
        functools.partial(_diff_attn_kernel, tq=tq, tk=tk, unroll=KEY_BLOCKS_PER_TRIP,
                          prep_rows=min(s, 512), lam_init=lam_init),
        grid=(b, N_ATTN_HEADS, s // tq),
        in_specs=[qspec, kvspec, kvspec, small(ATTN_HEAD_DIM), small(ATTN_HEAD_DIM),
                  small(ATTN_HEAD_DIM), small(ATTN_HEAD_DIM), small(LANES)],
        out_specs=qspec,
        out_shape=jax.ShapeDtypeStruct((b, s, ATTN_WIDTH), BF16),
        scratch_shapes=[pltpu.VMEM((s, 2 * LANES), BF16),
                        pltpu.VMEM((ACC_ROWS, s), BF16),
                        pltpu.VMEM((2, ACC_ROWS, tq), F32),
                        pltpu.VMEM((2, tk, tq), BF16)],
        compiler_params=pltpu.CompilerParams(
            dimension_semantics=("arbitrary", "arbitrary", "arbitrary"),
            vmem_limit_bytes=VMEM_LIMIT_BYTES),
        name="diff_attn",
    )(q, k, v, lq1, lk1, lq2, lk2, onorm)


def _split3(x):
    hi = x.astype(BF16)
    r = x - hi.astype(F32)
    mid = r.astype(BF16)
    lo = (r - mid.astype(F32)).astype(BF16)
    return hi, mid, lo


def _expand_heads(cols):
    rows = cols.shape[0]
    lane = lax.broadcasted_iota(jnp.int32, (rows, LANES), 1)
    parts = []
    for pr in range(N_SSM_HEADS // 2):
        a = jnp.broadcast_to(cols[:, 2 * pr:2 * pr + 1], (rows, LANES))
        b = jnp.broadcast_to(cols[:, 2 * pr + 1:2 * pr + 2], (rows, LANES))
        parts.append(jnp.where(lane < SSM_HEAD_DIM, a, b))
    return jnp.concatenate(parts, axis=1)


def _ssd_kernel(xbc_ref, z_ref, dt_ref, cw_ref, cb_ref, alog_ref, dskip_ref, on_ref, y_ref,
                xpad_ref, h_ref):
    c = pl.program_id(1)
    L = CHUNK
    halo = SUBLANES

    @pl.when(c == 0)
    def _():
        xpad_ref[0:halo, :] = jnp.zeros((halo, CONV_DIM), F32)
        h_ref[...] = jnp.zeros_like(h_ref)

    cur = xbc_ref[0]
    xpad_ref[halo:halo + L, :] = cur
    conv = cur * cw_ref[CONV_WIDTH - 1:CONV_WIDTH, :] + cb_ref[...]
    for j in range(CONV_WIDTH - 1):
        off = halo - (CONV_WIDTH - 1) + j
        conv = conv + xpad_ref[off:off + L, :] * cw_ref[j:j + 1, :]
    xpad_ref[0:halo, :] = cur[L - halo:, :]
    act = _silu(conv)
    xs = act[:, :SSM_WIDTH]

    dt = jax.nn.softplus(dt_ref[0])
    a = -jnp.exp(alog_ref[...])
    ad = dt * a

    r = lax.broadcasted_iota(jnp.int32, (L, L), 0)
    cc = lax.broadcasted_iota(jnp.int32, (L, L), 1)
    tril = r >= cc
    tri = jnp.where(tril, 1.0, 0.0).astype(BF16)
    hi, mid, lo = _split3(ad)
    a_cs = _dot(tri, hi) + _dot(tri, mid) + _dot(tri, lo)
    a_cs_t = a_cs.T

    dt_l = _expand_heads(dt)
    acs_l = _expand_heads(a_cs)
    last_l = acs_l[L - 1:L, :]
    xd = xs * dt_l
    xw = (xd * jnp.exp(last_l - acs_l)).astype(BF16)
    e_in = jnp.exp(acs_l)
    e_chunk_t = jnp.exp(a_cs_t[:, L - 1:L])
    xd16 = xd.astype(BF16)
    glane = lax.broadcasted_iota(jnp.int32, (L, GROUP_WIDTH), 1)

    ys = []
    for g in range(SSM_GROUPS):
        bg = act[:, SSM_WIDTH + g * SSM_STATE:SSM_WIDTH + (g + 1) * SSM_STATE].astype(BF16)
        cg = act[:, SSM_WIDTH + (SSM_GROUPS + g) * SSM_STATE:
                 SSM_WIDTH + (SSM_GROUPS + g + 1) * SSM_STATE].astype(BF16)
        gs = slice(g * GROUP_WIDTH, (g + 1) * GROUP_WIDTH)
        cb = _dot_nt(cg, bg)
        xd_g = xd16[:, gs]
        y_g = None
        for hl in range(HEADS_PER_GROUP):
            hd = g * HEADS_PER_GROUP + hl
            seg = a_cs[:, hd:hd + 1] - a_cs_t[hd:hd + 1, :]
            m = (cb * jnp.exp(jnp.where(tril, seg, NEG_BIG))).astype(BF16)
            in_head = (glane >= hl * SSM_HEAD_DIM) & (glane < (hl + 1) * SSM_HEAD_DIM)
            d = _dot(m, jnp.where(in_head, xd_g, jnp.zeros_like(xd_g)))
            y_g = d if y_g is None else y_g + d
        h_in = h_ref[g]
        y_off = _dot_nt(cg, h_in.astype(BF16)) * e_in[:, gs]
        ys.append(y_g + y_off)
        decay_rows = jnp.concatenate(
            [jnp.broadcast_to(e_chunk_t[g * HEADS_PER_GROUP + hl:g * HEADS_PER_GROUP + hl + 1, :],
                              (SSM_HEAD_DIM, SSM_STATE)) for hl in range(HEADS_PER_GROUP)], axis=0)
        h_ref[g] = h_in * decay_rows + _dot_tn(xw[:, gs], bg)

    y = jnp.concatenate(ys, axis=1) + dskip_ref[...] * xs
    y = y * _silu(z_ref[0])
    outs = []
    for g in range(SSM_GROUPS):
        gs = slice(g * GROUP_WIDTH, (g + 1) * GROUP_WIDTH)
        outs.append(_rms(y[:, gs], on_ref[:, gs]))
    y_ref[0] = jnp.concatenate(outs, axis=1).astype(BF16)


def _ssd(xbc, z, dt, cw, cb, alog, dskip, onorm):
    b, s, _ = xbc.shape
    blk = lambda w: pl.BlockSpec((1, CHUNK, w), lambda bi, ci: (bi, ci, 0))
    small = lambda r, w: pl.BlockSpec((r, w), lambda bi, ci: (0, 0))
    return pl.pallas_call(
        _ssd_kernel,
        grid=(b, s // CHUNK),
        in_specs=[blk(CONV_DIM), blk(SSM_WIDTH), blk(LANES), small(CONV_WIDTH, CONV_DIM),
                  small(1, CONV_DIM), small(1, LANES), small(1, SSM_WIDTH), small(1, SSM_WIDTH)],
        out_specs=blk(SSM_WIDTH),
        out_shape=jax.ShapeDtypeStruct((b, s, SSM_WIDTH), BF16),
        scratch_shapes=[pltpu.VMEM((SUBLANES + CHUNK, CONV_DIM), F32),
                        pltpu.VMEM((SSM_GROUPS, GROUP_WIDTH, SSM_STATE), F32)],
        compiler_params=pltpu.CompilerParams(dimension_semantics=("arbitrary", "arbitrary"),
                                             vmem_limit_bytes=VMEM_LIMIT_BYTES),
        name="ssd",
    )(xbc, z, dt, cw, cb, alog, dskip, onorm)


def _out_ffn_ple_kernel(x1_ref, oa_ref, ys_ref, p_ref, woa_ref, woy_ref, n2_ref, wg_ref, wu_ref,
                        wd_ref, ng_ref, wpg_ref, wpp_ref, npl_ref, out_ref, *, ff_chunk):
    x2 = x1_ref[...] + _dot(oa_ref[...], woa_ref[...]) + _dot(ys_ref[...], woy_ref[...])
    x3 = _swiglu_residual(x2, n2_ref[...], wg_ref, wu_ref, wd_ref, ff_chunk)
    e = _rms(_dot(p_ref[...].astype(BF16), wpp_ref[...]), npl_ref[...])
    gate = jax.nn.sigmoid(_dot(_rms(x3, ng_ref[...]).astype(BF16), wpg_ref[...]))
    out_ref[...] = x3 + gate * e


def _out_ffn_ple(x1, oa, ys, p, woa, woy, n2, wg, wu, wd, ng, wpg, wpp, npl, *, tm, ff_chunk):
    n, d = x1.shape
    d_ff = wg.shape[1]
    row = lambda w: pl.BlockSpec((tm, w), lambda i: (i, 0))
    return pl.pallas_call(
        functools.partial(_out_ffn_ple_kernel, ff_chunk=ff_chunk),
        grid=(n // tm,),
        in_specs=[row(d), row(ATTN_WIDTH), row(SSM_WIDTH), row(p.shape[1]),
                  _resident(woa.shape), _resident(woy.shape), _resident((1, d)),
                  _resident((d, d_ff)), _resident((d, d_ff)), _resident((d_ff, d)),
                  _resident((1, d)), _resident(wpg.shape), _resident(wpp.shape), _resident((1, d))],
        out_specs=row(d),
        out_shape=jax.ShapeDtypeStruct((n, d), F32),
        compiler_params=pltpu.CompilerParams(dimension_semantics=("arbitrary",),
                                             vmem_limit_bytes=VMEM_LIMIT_BYTES),
        name="out_ffn_ple",
    )(x1, oa, ys, p, woa, woy, n2, wg, wu, wd, ng, wpg, wpp, npl)


def _tiles(n, s, d_ff):
    tm = 512 if n % 512 == 0 else SUBLANES
    tq = 512 if s % 512 == 0 else CHUNK
    tk = tq
    ff_chunk = 6 * MXU_WIDTH
    return tm, tq, tk, ff_chunk


def _layer(i, x, p_i, prm):
    b, s, d = x.shape
    n = b * s
    d_ff = prm["ffn1_w_gate"].shape[-1]
    tm, tq, tk, ff_chunk = _tiles(n, s, d_ff)
    lam_init = 0.8 - 0.6 * math.exp(-0.3 * i)
    row = lambda v: v.reshape(1, -1).astype(F32)
    w16 = lambda w: w.astype(BF16)

    w_in = prm["w_in"]
    n_main = 3 * ATTN_WIDTH + SSM_WIDTH + CONV_DIM
    win = w16(w_in[:, :n_main])
    wdt = w16(jnp.pad(w_in[:, n_main:], ((0, 0), (0, LANES - N_SSM_HEADS))))
    dtb = jnp.pad(row(prm["dt_bias"]), ((0, 0), (0, LANES - N_SSM_HEADS)))
    alog = jnp.pad(row(prm["a_log"]), ((0, 0), (0, LANES - N_SSM_HEADS)))
    qg = jnp.tile(row(prm["q_norm"]), (1, 2)) * (ATTN_HEAD_DIM ** -0.5 * LOG2E)
    kg = jnp.tile(row(prm["k_norm"]), (1, 2))
    dskip = jnp.repeat(row(prm["d_skip"]), SSM_HEAD_DIM, axis=1)
    w_out = prm["w_out"]

    x1, q, k, v, z, xbc, dt = _ffn_inproj(
        x.reshape(n, d), row(prm["ffn1_norm"]), w16(prm["ffn1_w_gate"]), w16(prm["ffn1_w_up"]),
        w16(prm["ffn1_w_down"]), row(prm["mix_norm"]), win, wdt, qg, kg, dtb,
        tm=tm, ff_chunk=ff_chunk)

    r3 = lambda t: t.reshape(b, s, t.shape[-1])
    oa = _diff_attn(r3(q), r3(k), r3(v), row(prm["lambda_q1"]), row(prm["lambda_k1"]),
                    row(prm["lambda_q2"]), row(prm["lambda_k2"]), row(prm["attn_out_norm"]),
                    tq=tq, tk=tk, lam_init=lam_init)
    ys = _ssd(r3(xbc), r3(z), r3(dt), prm["conv_w"].astype(F32), row(prm["conv_b"]), alog, dskip,
              row(prm["ssm_out_norm"]))

    out = _out_ffn_ple(
        x1, oa.reshape(n, ATTN_WIDTH), ys.reshape(n, SSM_WIDTH), p_i.reshape(n, p_i.shape[-1]),
        w16(w_out[:ATTN_WIDTH]), w16(w_out[ATTN_WIDTH:]), row(prm["ffn2_norm"]),
        w16(prm["ffn2_w_gate"]), w16(prm["ffn2_w_up"]), w16(prm["ffn2_w_down"]),
        row(prm["ple_gate_norm"]), w16(prm["w_ple_gate"]), w16(prm["w_ple_proj"]),
        row(prm["ple_norm"]), tm=tm, ff_chunk=ff_chunk)
    return out.reshape(b, s, d)


def kernel(x, p, ffn1_norm, ffn1_w_gate, ffn1_w_up, ffn1_w_down, mix_norm, w_in, q_norm, k_norm, lambda_q1, lambda_k1, lambda_q2, lambda_k2, attn_out_norm, conv_w, conv_b, dt_bias, a_log, d_skip, ssm_out_norm, w_out, ffn2_norm, ffn2_w_gate, ffn2_w_up, ffn2_w_down, ple_gate_norm, w_ple_gate, w_ple_proj, ple_norm):
    stacked = dict(
        ffn1_norm=ffn1_norm, ffn1_w_gate=ffn1_w_gate, ffn1_w_up=ffn1_w_up, ffn1_w_down=ffn1_w_down,
        mix_norm=mix_norm, w_in=w_in, q_norm=q_norm, k_norm=k_norm, lambda_q1=lambda_q1,
        lambda_k1=lambda_k1, lambda_q2=lambda_q2, lambda_k2=lambda_k2, attn_out_norm=attn_out_norm,
        conv_w=conv_w, conv_b=conv_b, dt_bias=dt_bias, a_log=a_log, d_skip=d_skip,
        ssm_out_norm=ssm_out_norm, w_out=w_out, ffn2_norm=ffn2_norm, ffn2_w_gate=ffn2_w_gate,
        ffn2_w_up=ffn2_w_up, ffn2_w_down=ffn2_w_down, ple_gate_norm=ple_gate_norm,
        w_ple_gate=w_ple_gate, w_ple_proj=w_ple_proj, ple_norm=ple_norm)
    for i in range(p.shape[0]):
        x = _layer(i, x, p[i], {name: w[i] for name, w in stacked.items()})
    return x
```

```python
import functools
import math

import numpy as np
import jax
import jax.numpy as jnp
from jax import lax
from jax.experimental import pallas as pl
from jax.experimental.pallas import tpu as pltpu

N_ATTN_HEADS = 4
ATTN_HEAD_DIM = 64
SSM_HEAD_DIM = 64
N_SSM_HEADS = 8
SSM_GROUPS = 2
SSM_STATE = 128
CONV_WIDTH = 4
CHUNK = 256
NORM_EPS = 1e-6

LANES = 128
SUBLANES = 8
MXU_WIDTH = 256
VMEM_LIMIT_BYTES = 56 * 1024 * 1024

ATTN_WIDTH = 2 * N_ATTN_HEADS * ATTN_HEAD_DIM
SSM_WIDTH = N_SSM_HEADS * SSM_HEAD_DIM
HEADS_PER_GROUP = N_SSM_HEADS // SSM_GROUPS
GROUP_WIDTH = HEADS_PER_GROUP * SSM_HEAD_DIM
CONV_DIM = SSM_WIDTH + 2 * SSM_GROUPS * SSM_STATE
NEG_BIG = -1e30

BF16 = jnp.bfloat16
F32 = jnp.float32


def _rms(x, g):
    ms = jnp.mean(x * x, axis=-1, keepdims=True)
    return x * lax.rsqrt(ms + NORM_EPS) * g


def _silu(x):
    return x * jax.nn.sigmoid(x)


def _dot(a, b):
    return jnp.dot(a, b, preferred_element_type=F32)


def _dot_nt(a, b):
    return lax.dot_general(a, b, (((1,), (1,)), ((), ())), preferred_element_type=F32)


def _dot_tn(a, b):
    return lax.dot_general(a, b, (((0,), (0,)), ((), ())), preferred_element_type=F32)


def _resident(shape):
    zeros = (0,) * len(shape)
    return pl.BlockSpec(shape, lambda *_: zeros, pipeline_mode=pl.Buffered(1))


def _swiglu_residual(x, norm, wg_ref, wu_ref, wd_ref, ff_chunk):
    h = _rms(x, norm).astype(BF16)
    d_ff = wg_ref.shape[1]
    acc = None
    for lo in range(0, d_ff, ff_chunk):
        hi = min(lo + ff_chunk, d_ff)
        g = _dot(h, wg_ref[:, lo:hi])
        u = _dot(h, wu_ref[:, lo:hi])
        a = (_silu(g) * u).astype(BF16)
        d = _dot(a, wd_ref[lo:hi, :])
        acc = d if acc is None else acc + d
    return x + 0.5 * acc


def _pair_rms(x, g):
    lane = lax.broadcasted_iota(jnp.int32, x.shape, 1)
    lo = lane < ATTN_HEAD_DIM
    sq = x * x
    s_all = jnp.sum(sq, axis=-1, keepdims=True)
    s_lo = jnp.sum(jnp.where(lo, sq, 0.0), axis=-1, keepdims=True)
    s_hi = s_all - s_lo
    inv = 1.0 / ATTN_HEAD_DIM
    r = jnp.where(lo, lax.rsqrt(s_lo * inv + NORM_EPS), lax.rsqrt(s_hi * inv + NORM_EPS))
    return x * r * g


def _ffn_inproj_kernel(x_ref, n1_ref, wg_ref, wu_ref, wd_ref, nm_ref, win_ref, wdt_ref,
                       qg_ref, kg_ref, dtb_ref,
                       x1_ref, q_ref, k_ref, v_ref, z_ref, xbc_ref, dt_ref, *, ff_chunk):
    x1 = _swiglu_residual(x_ref[...], n1_ref[...], wg_ref, wu_ref, wd_ref, ff_chunk)
    x1_ref[...] = x1
    h = _rms(x1, nm_ref[...]).astype(BF16)
    aw = ATTN_WIDTH
    q = _dot(h, win_ref[:, 0:aw])
    k = _dot(h, win_ref[:, aw:2 * aw])
    for hd in range(N_ATTN_HEADS):
        sl = slice(hd * LANES, (hd + 1) * LANES)
        q_ref[:, sl] = _pair_rms(q[:, sl], qg_ref[...]).astype(BF16)
        k_ref[:, sl] = _pair_rms(k[:, sl], kg_ref[...]).astype(BF16)
    v_ref[...] = _dot(h, win_ref[:, 2 * aw:3 * aw]).astype(BF16)
    z_ref[...] = _dot(h, win_ref[:, 3 * aw:3 * aw + SSM_WIDTH])
    xbc_ref[...] = _dot(h, win_ref[:, 3 * aw + SSM_WIDTH:])
    dt_ref[...] = _dot(h, wdt_ref[...]) + dtb_ref[...]


def _ffn_inproj(x, n1, wg, wu, wd, nm, win, wdt, qg, kg, dtb, *, tm, ff_chunk):
    n, d = x.shape
    d_ff = wg.shape[1]
    row = lambda w: pl.BlockSpec((tm, w), lambda i: (i, 0))
    out_shape = (
        jax.ShapeDtypeStruct((n, d), F32),
        jax.ShapeDtypeStruct((n, ATTN_WIDTH), BF16),
        jax.ShapeDtypeStruct((n, ATTN_WIDTH), BF16),
        jax.ShapeDtypeStruct((n, ATTN_WIDTH), BF16),
        jax.ShapeDtypeStruct((n, SSM_WIDTH), F32),
        jax.ShapeDtypeStruct((n, CONV_DIM), F32),
        jax.ShapeDtypeStruct((n, LANES), F32),
    )
    return pl.pallas_call(
        functools.partial(_ffn_inproj_kernel, ff_chunk=ff_chunk),
        grid=(n // tm,),
        in_specs=[row(d), _resident((1, d)), _resident((d, d_ff)), _resident((d, d_ff)),
                  _resident((d_ff, d)), _resident((1, d)), _resident(win.shape), _resident(wdt.shape),
                  _resident((1, LANES)), _resident((1, LANES)), _resident((1, LANES))],
        out_specs=(row(d), row(ATTN_WIDTH), row(ATTN_WIDTH), row(ATTN_WIDTH), row(SSM_WIDTH),
                   row(CONV_DIM), row(LANES)),
        out_shape=out_shape,
        compiler_params=pltpu.CompilerParams(dimension_semantics=("arbitrary",),
                                             vmem_limit_bytes=VMEM_LIMIT_BYTES),
        name="ffn_inproj",
    )(x, n1, wg, wu, wd, nm, win, wdt, qg, kg, dtb)


def _bf16_terms(x, n):
    terms = []
    for _ in range(n):
        t = float(np.asarray(x, dtype=BF16))
        terms.append(t)
        x -= t
    return terms


LOG2E = math.log2(math.e)
LOG2E_TERMS = _bf16_terms(LOG2E, 3)
POS_SPLIT = 128
ACC_ROWS = LANES + 16
KEY_BLOCKS_PER_TRIP = 4
MAX_STALE_EXCESS = 64.0


def _diff_attn_kernel(q_ref, k_ref, v_ref, lq1_ref, lk1_ref, lq2_ref, lk2_ref, on_ref, o_ref,
                      kaug_ref, vt_ref, acc_ref, p_ref, *, tq, tk, unroll, prep_rows, lam_init):
    hd = pl.program_id(1)
    qi = pl.program_id(2)
    s_len = k_ref.shape[1]
    n_terms = len(LOG2E_TERMS)

    @pl.when(qi == 0)
    def _():
        slope_bits = (127 - (8 // N_ATTN_HEADS) * (hd + 1)) << 23
        slope = lax.bitcast_convert_type(jnp.full((prep_rows, LANES), slope_bits, jnp.int32), F32)
        lane = lax.broadcasted_iota(jnp.int32, (prep_rows, LANES), 1)
        rowi = lax.broadcasted_iota(jnp.int32, (prep_rows, LANES), 0)

        def prep(t, _):
            start = pl.multiple_of(t * prep_rows, prep_rows)
            pos = start + rowi
            lo = pos & (POS_SPLIT - 1)
            hi_part = slope * (pos - lo).astype(F32)
            lo_part = slope * lo.astype(F32)
            cols = jnp.where(lane < n_terms, hi_part, jnp.where(lane < 2 * n_terms, lo_part, 0.0))
            kaug_ref[pl.ds(start, prep_rows), 0:LANES] = k_ref[0, pl.ds(start, prep_rows), :]
            kaug_ref[pl.ds(start, prep_rows), LANES:2 * LANES] = cols.astype(BF16)
            vt = v_ref[0, pl.ds(start, prep_rows), :].astype(F32).T
            vt_ref[0:LANES, pl.ds(start, prep_rows)] = vt.astype(BF16)
            return 0

        lax.fori_loop(0, s_len // prep_rows, prep, 0)
        pad_row = lax.broadcasted_iota(jnp.int32, (ACC_ROWS - LANES, s_len), 0)
        vt_ref[LANES:ACC_ROWS, :] = jnp.where(pad_row == 0, 1.0, 0.0).astype(BF16)

    q = q_ref[0]
    lane = lax.broadcasted_iota(jnp.int32, q.shape, 1)
    consts = jnp.zeros(q.shape, F32)
    for t, term in enumerate(LOG2E_TERMS):
        consts = jnp.where((lane == t) | (lane == n_terms + t), term, consts)
    consts = consts.astype(BF16)
    zero = jnp.zeros_like(q)
    qaug = (jnp.concatenate([jnp.where(lane < ATTN_HEAD_DIM, q, zero), consts], axis=1),
            jnp.concatenate([jnp.where(lane >= ATTN_HEAD_DIM, q, zero), consts], axis=1))

    key = lax.broadcasted_iota(jnp.int32, (tk, tq), 0)
    qry = lax.broadcasted_iota(jnp.int32, (tk, tq), 1)
    def scores(j):
        kb = kaug_ref[pl.ds(pl.multiple_of(j * tk, tk), tk), :]
        return [_dot_nt(kb, qaug[c]) for c in range(2)]

    def softmax_block(ss, ms, masked, stale):
        new, coefs, ps, excess = [], [], [], []
        for c in range(2):
            s = ss[c]
            if masked:
                s = jnp.where(key <= qry, s, NEG_BIG)
            block_max = jnp.max(s, axis=0, keepdims=True)
            m_new = jnp.maximum(ms[c], block_max)
            ps.append(jnp.exp2(s - (ms[c] if stale else m_new)).astype(BF16))
            rescale = jnp.exp2(ms[c] - m_new)
            coefs.append((rescale, rescale if stale else jnp.ones_like(rescale)))
            new.append(m_new)
            excess.append(block_max - ms[c])
        return tuple(new), tuple(coefs), ps, jnp.maximum(excess[0], excess[1])

    def accumulate(j, coefs, ps):
        vb = vt_ref[:, pl.ds(pl.multiple_of(j * tk, tk), tk)]
        for c in range(2):
            a, b = coefs[c]
            acc_ref[c] = a * acc_ref[c] + b * _dot(vb, ps[c])

    def attend(stale, blocks_per_trip):
        def trip(n_blocks, stale):
            def body(t, carry):
                ms, coefs, worst = carry
                ps = [p_ref[0], p_ref[1]]
                for u in range(n_blocks):
                    j = t * n_blocks + u
                    ss = scores(j)
                    accumulate(jnp.where(j == 0, qi, j - 1), coefs, ps)
                    ms, coefs, ps, excess = softmax_block(ss, ms, False, stale)
                    if stale:
                        worst = jnp.maximum(worst, excess)
                p_ref[0], p_ref[1] = ps
                return ms, coefs, worst
            return body

        acc_ref[...] = jnp.zeros_like(acc_ref)
        m0 = jnp.full((1, tq), NEG_BIG, F32)
        ms, coefs, ps, _ = softmax_block(scores(qi), (m0, m0), True, False)
        p_ref[0], p_ref[1] = ps
        n_trips = qi // blocks_per_trip
        carry = lax.fori_loop(0, n_trips, trip(blocks_per_trip, stale), (ms, coefs, m0))
        _, coefs, worst = lax.fori_loop(n_trips * blocks_per_trip, qi, trip(1, False), carry)
        accumulate(jnp.where(qi == 0, 0, qi - 1), coefs, [p_ref[0], p_ref[1]])
        return worst

    worst = attend(True, unroll)

    @pl.when(jnp.max(worst) > MAX_STALE_EXCESS)
    def _():
        attend(False, 1)

    lam = (jnp.exp(jnp.sum(lq1_ref[...] * lk1_ref[...], axis=-1, keepdims=True))
           - jnp.exp(jnp.sum(lq2_ref[...] * lk2_ref[...], axis=-1, keepdims=True)) + lam_init)
    a1, a2 = acc_ref[0], acc_ref[1]
    o_t = a1[0:LANES] / a1[LANES:LANES + 1] - lam * (a2[0:LANES] / a2[LANES:LANES + 1])
    o_ref[0] = (_rms(o_t.T, on_ref[...]) * (1.0 - lam_init)).astype(BF16)


def _diff_attn(q, k, v, lq1, lk1, lq2, lk2, onorm, *, tq, tk, lam_init):
    b, s, _ = q.shape
    assert tq == tk, "the causal mask assumes square score blocks"
    qspec = pl.BlockSpec((1, tq, LANES), lambda bi, hi, qi: (bi, qi, hi))
    kvspec = pl.BlockSpec((1, s, LANES), lambda bi, hi, qi: (bi, 0, hi))
    small = lambda w: pl.BlockSpec((1, w), lambda bi, hi, qi: (0, 0))
    return pl.pallas_call(
        functools.partial(_diff_attn_kernel, tq=tq, tk=tk, unroll=KEY_BLOCKS_PER_TRIP,
                          prep_rows=min(s, 512), lam_init=lam_init),
        grid=(b, N_ATTN_HEADS, s // tq),
        in_specs=[qspec, kvspec, kvspec, small(ATTN_HEAD_DIM), small(ATTN_HEAD_DIM),
                  small(ATTN_HEAD_DIM), small(ATTN_HEAD_DIM), small(LANES)],
        out_specs=qspec,
        out_shape=jax.ShapeDtypeStruct((b, s, ATTN_WIDTH), BF16),
        scratch_shapes=[pltpu.VMEM((s, 2 * LANES), BF16),
                        pltpu.VMEM((ACC_ROWS, s), BF16),
                        pltpu.VMEM((2, ACC_ROWS, tq), F32),
                        pltpu.VMEM((2, tk, tq), BF16)],
        compiler_params=pltpu.CompilerParams(
            dimension_semantics=("arbitrary", "arbitrary", "arbitrary"),
            vmem_limit_bytes=VMEM_LIMIT_BYTES),
        name="diff_attn",
    )(q, k, v, lq1, lk1, lq2, lk2, onorm)


def _split3(x):
    hi = x.astype(BF16)
    r = x - hi.astype(F32)
    mid = r.astype(BF16)
    lo = (r - mid.astype(F32)).astype(BF16)
    return hi, mid, lo


def _expand_heads(cols):
    rows = cols.shape[0]
    lane = lax.broadcasted_iota(jnp.int32, (rows, LANES), 1)
    parts = []
    for pr in range(N_SSM_HEADS // 2):
        a = jnp.broadcast_to(cols[:, 2 * pr:2 * pr + 1], (rows, LANES))
        b = jnp.broadcast_to(cols[:, 2 * pr + 1:2 * pr + 2], (rows, LANES))
        parts.append(jnp.where(lane < SSM_HEAD_DIM, a, b))
    return jnp.concatenate(parts, axis=1)


def _ssd_kernel(xbc_ref, z_ref, dt_ref, cw_ref, cb_ref, alog_ref, dskip_ref, on_ref, y_ref,
                xpad_ref, h_ref):
    c = pl.program_id(1)
    L = CHUNK
    halo = SUBLANES

    @pl.when(c == 0)
    def _():
        xpad_ref[0:halo, :] = jnp.zeros((halo, CONV_DIM), F32)
        h_ref[...] = jnp.zeros_like(h_ref)

    cur = xbc_ref[0]
    xpad_ref[halo:halo + L, :] = cur
    conv = cur * cw_ref[CONV_WIDTH - 1:CONV_WIDTH, :] + cb_ref[...]
    for j in range(CONV_WIDTH - 1):
        off = halo - (CONV_WIDTH - 1) + j
        conv = conv + xpad_ref[off:off + L, :] * cw_ref[j:j + 1, :]
    xpad_ref[0:halo, :] = cur[L - halo:, :]
    act = _silu(conv)
    xs = act[:, :SSM_WIDTH]

    dt = jax.nn.softplus(dt_ref[0])
    a = -jnp.exp(alog_ref[...])
    ad = dt * a

    r = lax.broadcasted_iota(jnp.int32, (L, L), 0)
    cc = lax.broadcasted_iota(jnp.int32, (L, L), 1)
    tril = r >= cc
    tri = jnp.where(tril, 1.0, 0.0).astype(BF16)
    hi, mid, lo = _split3(ad)
    a_cs = _dot(tri, hi) + _dot(tri, mid) + _dot(tri, lo)
    a_cs_t = a_cs.T

    dt_l = _expand_heads(dt)
    acs_l = _expand_heads(a_cs)
    last_l = acs_l[L - 1:L, :]
    xd = xs * dt_l
    xw = (xd * jnp.exp(last_l - acs_l)).astype(BF16)
    e_in = jnp.exp(acs_l)
    e_chunk_t = jnp.exp(a_cs_t[:, L - 1:L])
    xd16 = xd.astype(BF16)
    glane = lax.broadcasted_iota(jnp.int32, (L, GROUP_WIDTH), 1)

    ys = []
    for g in range(SSM_GROUPS):
        bg = act[:, SSM_WIDTH + g * SSM_STATE:SSM_WIDTH + (g + 1) * SSM_STATE].astype(BF16)
        cg = act[:, SSM_WIDTH + (SSM_GROUPS + g) * SSM_STATE:
                 SSM_WIDTH + (SSM_GROUPS + g + 1) * SSM_STATE].astype(BF16)
        gs = slice(g * GROUP_WIDTH, (g + 1) * GROUP_WIDTH)
        cb = _dot_nt(cg, bg)
        xd_g = xd16[:, gs]
        y_g = None
        for hl in range(HEADS_PER_GROUP):
            hd = g * HEADS_PER_GROUP + hl
            seg = a_cs[:, hd:hd + 1] - a_cs_t[hd:hd + 1, :]
            m = (cb * jnp.exp(jnp.where(tril, seg, NEG_BIG))).astype(BF16)
            in_head = (glane >= hl * SSM_HEAD_DIM) & (glane < (hl + 1) * SSM_HEAD_DIM)
            d = _dot(m, jnp.where(in_head, xd_g, jnp.zeros_like(xd_g)))
            y_g = d if y_g is None else y_g + d
        h_in = h_ref[g]
        y_off = _dot_nt(cg, h_in.astype(BF16)) * e_in[:, gs]
        ys.append(y_g + y_off)
        decay_rows = jnp.concatenate(
            [jnp.broadcast_to(e_chunk_t[g * HEADS_PER_GROUP + hl:g * HEADS_PER_GROUP + hl + 1, :],
                              (SSM_HEAD_DIM, SSM_STATE)) for hl in range(HEADS_PER_GROUP)], axis=0)
        h_ref[g] = h_in * decay_rows + _dot_tn(xw[:, gs], bg)

    y = jnp.concatenate(ys, axis=1) + dskip_ref[...] * xs
    y = y * _silu(z_ref[0])
    outs = []
    for g in range(SSM_GROUPS):
        gs = slice(g * GROUP_WIDTH, (g + 1) * GROUP_WIDTH)
        outs.append(_rms(y[:, gs], on_ref[:, gs]))
    y_ref[0] = jnp.concatenate(outs, axis=1).astype(BF16)


def _ssd(xbc, z, dt, cw, cb, alog, dskip, onorm):
    b, s, _ = xbc.shape
    blk = lambda w: pl.BlockSpec((1, CHUNK, w), lambda bi, ci: (bi, ci, 0))
    small = lambda r, w: pl.BlockSpec((r, w), lambda bi, ci: (0, 0))
    return pl.pallas_call(
        _ssd_kernel,
        grid=(b, s // CHUNK),
        in_specs=[blk(CONV_DIM), blk(SSM_WIDTH), blk(LANES), small(CONV_WIDTH, CONV_DIM),
                  small(1, CONV_DIM), small(1, LANES), small(1, SSM_WIDTH), small(1, SSM_WIDTH)],
        out_specs=blk(SSM_WIDTH),
        out_shape=jax.ShapeDtypeStruct((b, s, SSM_WIDTH), BF16),
        scratch_shapes=[pltpu.VMEM((SUBLANES + CHUNK, CONV_DIM), F32),
                        pltpu.VMEM((SSM_GROUPS, GROUP_WIDTH, SSM_STATE), F32)],
        compiler_params=pltpu.CompilerParams(dimension_semantics=("arbitrary", "arbitrary"),
                                             vmem_limit_bytes=VMEM_LIMIT_BYTES),
        name="ssd",
    )(xbc, z, dt, cw, cb, alog, dskip, onorm)


def _out_ffn_ple_kernel(x1_ref, oa_ref, ys_ref, p_ref, woa_ref, woy_ref, n2_ref, wg_ref, wu_ref,
                        wd_ref, ng_ref, wpg_ref, wpp_ref, npl_ref, out_ref, *, ff_chunk):
    x2 = x1_ref[...] + _dot(oa_ref[...], woa_ref[...]) + _dot(ys_ref[...], woy_ref[...])
    x3 = _swiglu_residual(x2, n2_ref[...], wg_ref, wu_ref, wd_ref, ff_chunk)
    e = _rms(_dot(p_ref[...].astype(BF16), wpp_ref[...]), npl_ref[...])
    gate = jax.nn.sigmoid(_dot(_rms(x3, ng_ref[...]).astype(BF16), wpg_ref[...]))
    out_ref[...] = x3 + gate * e


def _out_ffn_ple(x1, oa, ys, p, woa, woy, n2, wg, wu, wd, ng, wpg, wpp, npl, *, tm, ff_chunk):
    n, d = x1.shape
    d_ff = wg.shape[1]
    row = lambda w: pl.BlockSpec((tm, w), lambda i: (i, 0))
    return pl.pallas_call(
        functools.partial(_out_ffn_ple_kernel, ff_chunk=ff_chunk),
        grid=(n // tm,),
        in_specs=[row(d), row(ATTN_WIDTH), row(SSM_WIDTH), row(p.shape[1]),
                  _resident(woa.shape), _resident(woy.shape), _resident((1, d)),
                  _resident((d, d_ff)), _resident((d, d_ff)), _resident((d_ff, d)),
                  _resident((1, d)), _resident(wpg.shape), _resident(wpp.shape), _resident((1, d))],
        out_specs=row(d),
        out_shape=jax.ShapeDtypeStruct((n, d), F32),
        compiler_params=pltpu.CompilerParams(dimension_semantics=("arbitrary",),
                                             vmem_limit_bytes=VMEM_LIMIT_BYTES),
        name="out_ffn_ple",
    )(x1, oa, ys, p, woa, woy, n2, wg, wu, wd, ng, wpg, wpp, npl)


def _tiles(n, s, d_ff):
    tm = 512 if n % 512 == 0 else SUBLANES
    tq = 512 if s % 512 == 0 else CHUNK
    tk = tq
    ff_chunk = 6 * MXU_WIDTH
    return tm, tq, tk, ff_chunk


def _layer(i, x, p_i, prm):
    b, s, d = x.shape
    n = b * s
    d_ff = prm["ffn1_w_gate"].shape[-1]
    tm, tq, tk, ff_chunk = _tiles(n, s, d_ff)
    lam_init = 0.8 - 0.6 * math.exp(-0.3 * i)
    row = lambda v: v.reshape(1, -1).astype(F32)
    w16 = lambda w: w.astype(BF16)

    w_in = prm["w_in"]
    n_main = 3 * ATTN_WIDTH + SSM_WIDTH + CONV_DIM
    win = w16(w_in[:, :n_main])
    wdt = w16(jnp.pad(w_in[:, n_main:], ((0, 0), (0, LANES - N_SSM_HEADS))))
    dtb = jnp.pad(row(prm["dt_bias"]), ((0, 0), (0, LANES - N_SSM_HEADS)))
    alog = jnp.pad(row(prm["a_log"]), ((0, 0), (0, LANES - N_SSM_HEADS)))
    qg = jnp.tile(row(prm["q_norm"]), (1, 2)) * (ATTN_HEAD_DIM ** -0.5 * LOG2E)
    kg = jnp.tile(row(prm["k_norm"]), (1, 2))
    dskip = jnp.repeat(row(prm["d_skip"]), SSM_HEAD_DIM, axis=1)
    w_out = prm["w_out"]

    x1, q, k, v, z, xbc, dt = _ffn_inproj(
        x.reshape(n, d), row(prm["ffn1_norm"]), w16(prm["ffn1_w_gate"]), w16(prm["ffn1_w_up"]),
        w16(prm["ffn1_w_down"]), row(prm["mix_norm"]), win, wdt, qg, kg, dtb,
        tm=tm, ff_chunk=ff_chunk)

    r3 = lambda t: t.reshape(b, s, t.shape[-1])
    oa = _diff_attn(r3(q), r3(k), r3(v), row(prm["lambda_q1"]), row(prm["lambda_k1"]),
                    row(prm["lambda_q2"]), row(prm["lambda_k2"]), row(prm["attn_out_norm"]),
                    tq=tq, tk=tk, lam_init=lam_init)
    ys = _ssd(r3(xbc), r3(z), r3(dt), prm["conv_w"].astype(F32), row(prm["conv_b"]), alog, dskip,
              row(prm["ssm_out_norm"]))

    out = _out_ffn_ple(
        x1, oa.reshape(n, ATTN_WIDTH), ys.reshape(n, SSM_WIDTH), p_i.reshape(n, p_i.shape[-1]),
        w16(w_out[:ATTN_WIDTH]), w16(w_out[ATTN_WIDTH:]), row(prm["ffn2_norm"]),
        w16(prm["ffn2_w_gate"]), w16(prm["ffn2_w_up"]), w16(prm["ffn2_w_down"]),
        row(prm["ple_gate_norm"]), w16(prm["w_ple_gate"]), w16(prm["w_ple_proj"]),
        row(prm["ple_norm"]), tm=tm, ff_chunk=ff_chunk)
    return out.reshape(b, s, d)


def kernel(x, p, ffn1_norm, ffn1_w_gate, ffn1_w_up, ffn1_w_down, mix_norm, w_in, q_norm, k_norm, lambda_q1, lambda_k1, lambda_q2, lambda_k2, attn_out_norm, conv_w, conv_b, dt_bias, a_log, d_skip, ssm_out_norm, w_out, ffn2_norm, ffn2_w_gate, ffn2_w_up, ffn2_w_down, ple_gate_norm, w_ple_gate, w_ple_proj, ple_norm):
    stacked = dict(
        ffn1_norm=ffn1_norm, ffn1_w_gate=ffn1_w_gate, ffn1_w_up=ffn1_w_up, ffn1_w_down=ffn1_w_down,
        mix_norm=mix_norm, w_in=w_in, q_norm=q_norm, k_norm=k_norm, lambda_q1=lambda_q1,
        lambda_k1=lambda_k1, lambda_q2=lambda_q2, lambda_k2=lambda_k2, attn_out_norm=attn_out_norm,
        conv_w=conv_w, conv_b=conv_b, dt_bias=dt_bias, a_log=a_log, d_skip=d_skip,
        ssm_out_norm=ssm_out_norm, w_out=w_out, ffn2_norm=ffn2_norm, ffn2_w_gate=ffn2_w_gate,
        ffn2_w_up=ffn2_w_up, ffn2_w_down=ffn2_w_down, ple_gate_norm=ple_gate_norm,
        w_ple_gate=w_ple_gate, w_ple_proj=w_ple_proj, ple_norm=ple_norm)
    for i in range(p.shape[0]):
        x = _layer(i, x, p[i], {name: w[i] for name, w in stacked.items()})
    return x
```

```python
import functools
import math

import numpy as np
import jax
import jax.numpy as jnp
from jax import lax
from jax.experimental import pallas as pl
from jax.experimental.pallas import tpu as pltpu

N_ATTN_HEADS = 4
ATTN_HEAD_DIM = 64
SSM_HEAD_DIM = 64
N_SSM_HEADS = 8
SSM_GROUPS = 2
SSM_STATE = 128
CONV_WIDTH = 4
CHUNK = 256
NORM_EPS = 1e-6

LANES = 128
SUBLANES = 8
MXU_WIDTH = 256
ROW_PARTS = 2
VMEM_LIMIT_BYTES = 56 * 1024 * 1024

ATTN_WIDTH = 2 * N_ATTN_HEADS * ATTN_HEAD_DIM
SSM_WIDTH = N_SSM_HEADS * SSM_HEAD_DIM
HEADS_PER_GROUP = N_SSM_HEADS // SSM_GROUPS
GROUP_WIDTH = HEADS_PER_GROUP * SSM_HEAD_DIM
CONV_DIM = SSM_WIDTH + 2 * SSM_GROUPS * SSM_STATE
NEG_BIG = -1e30

BF16 = jnp.bfloat16
F32 = jnp.float32


def _rms(x, g):
    ms = jnp.mean(x * x, axis=-1, keepdims=True)
    return x * lax.rsqrt(ms + NORM_EPS) * g


def _silu(x):
    return x * jax.nn.sigmoid(x)


def _dot(a, b):
    return jnp.dot(a, b, preferred_element_type=F32)


def _dot_nt(a, b):
    return lax.dot_general(a, b, (((1,), (1,)), ((), ())), preferred_element_type=F32)


def _dot_tn(a, b):
    return lax.dot_general(a, b, (((0,), (0,)), ((), ())), preferred_element_type=F32)


def _resident(shape):
    zeros = (0,) * len(shape)
    return pl.BlockSpec(shape, lambda *_: zeros, pipeline_mode=pl.Buffered(1))


def _swiglu_residual(x, norm, wg_ref, wu_ref, wd_ref, ff_chunk):
    h = _rms(x, norm).astype(BF16)
    d_ff = wg_ref.shape[1]
    acc = None
    for lo in range(0, d_ff, ff_chunk):
        hi = min(lo + ff_chunk, d_ff)
        g = _dot(h, wg_ref[:, lo:hi])
        u = _dot(h, wu_ref[:, lo:hi])
        a = (_silu(g) * u).astype(BF16)
        d = _dot(a, wd_ref[lo:hi, :])
        acc = d if acc is None else acc + d
    return x + 0.5 * acc


def _pair_rms(x, g):
    lane = lax.broadcasted_iota(jnp.int32, x.shape, 1)
    lo = lane < ATTN_HEAD_DIM
    sq = x * x
    s_all = jnp.sum(sq, axis=-1, keepdims=True)
    s_lo = jnp.sum(jnp.where(lo, sq, 0.0), axis=-1, keepdims=True)
    s_hi = s_all - s_lo
    inv = 1.0 / ATTN_HEAD_DIM
    r = jnp.where(lo, lax.rsqrt(s_lo * inv + NORM_EPS), lax.rsqrt(s_hi * inv + NORM_EPS))
    return x * r * g


def _ffn_inproj_kernel(x_ref, n1_ref, wg_ref, wu_ref, wd_ref, nm_ref, win_ref, wdt_ref,
                       qg_ref, kg_ref, dtb_ref,
                       x1_ref, q_ref, k_ref, v_ref, z_ref, xbc_ref, dt_ref, *, ff_chunk):
    aw = ATTN_WIDTH
    part = x_ref.shape[0] // ROW_PARTS
    for r in (slice(i * part, (i + 1) * part) for i in range(ROW_PARTS)):
        x1 = _swiglu_residual(x_ref[r, :], n1_ref[...], wg_ref, wu_ref, wd_ref, ff_chunk)
        x1_ref[r, :] = x1
        h = _rms(x1, nm_ref[...]).astype(BF16)
        q = _dot(h, win_ref[:, 0:aw])
        k = _dot(h, win_ref[:, aw:2 * aw])
        for hd in range(N_ATTN_HEADS):
            sl = slice(hd * LANES, (hd + 1) * LANES)
            q_ref[r, sl] = _pair_rms(q[:, sl], qg_ref[...]).astype(BF16)
            k_ref[r, sl] = _pair_rms(k[:, sl], kg_ref[...]).astype(BF16)
        v_ref[r, :] = _dot(h, win_ref[:, 2 * aw:3 * aw]).astype(BF16)
        z_ref[r, :] = _dot(h, win_ref[:, 3 * aw:3 * aw + SSM_WIDTH])
        xbc_ref[r, :] = _dot(h, win_ref[:, 3 * aw + SSM_WIDTH:])
        dt_ref[r, :] = _dot(h, wdt_ref[...]) + dtb_ref[...]


def _ffn_inproj(x, n1, wg, wu, wd, nm, win, wdt, qg, kg, dtb, *, tm, ff_chunk):
    n, d = x.shape
    d_ff = wg.shape[1]
    row = lambda w: pl.BlockSpec((tm, w), lambda i: (i, 0))
    out_shape = (
        jax.ShapeDtypeStruct((n, d), F32),
        jax.ShapeDtypeStruct((n, ATTN_WIDTH), BF16),
        jax.ShapeDtypeStruct((n, ATTN_WIDTH), BF16),
        jax.ShapeDtypeStruct((n, ATTN_WIDTH), BF16),
        jax.ShapeDtypeStruct((n, SSM_WIDTH), F32),
        jax.ShapeDtypeStruct((n, CONV_DIM), F32),
        jax.ShapeDtypeStruct((n, LANES), F32),
    )
    return pl.pallas_call(
        functools.partial(_ffn_inproj_kernel, ff_chunk=ff_chunk),
        grid=(n // tm,),
        in_specs=[row(d), _resident((1, d)), _resident((d, d_ff)), _resident((d, d_ff)),
                  _resident((d_ff, d)), _resident((1, d)), _resident(win.shape), _resident(wdt.shape),
                  _resident((1, LANES)), _resident((1, LANES)), _resident((1, LANES))],
        out_specs=(row(d), row(ATTN_WIDTH), row(ATTN_WIDTH), row(ATTN_WIDTH), row(SSM_WIDTH),
                   row(CONV_DIM), row(LANES)),
        out_shape=out_shape,
        compiler_params=pltpu.CompilerParams(dimension_semantics=("arbitrary",),
                                             vmem_limit_bytes=VMEM_LIMIT_BYTES),
        name="ffn_inproj",
    )(x, n1, wg, wu, wd, nm, win, wdt, qg, kg, dtb)


def _bf16_terms(x, n):
    terms = []
    for _ in range(n):
        t = float(np.asarray(x, dtype=BF16))
        terms.append(t)
        x -= t
    return terms


LOG2E = math.log2(math.e)
LOG2E_TERMS = _bf16_terms(LOG2E, 3)
POS_SPLIT = 128
ACC_ROWS = LANES + 16
KEY_BLOCKS_PER_TRIP = 4
MAX_STALE_EXCESS = 64.0


def _diff_attn_kernel(q_ref, k_ref, v_ref, lq1_ref, lk1_ref, lq2_ref, lk2_ref, on_ref, o_ref,
                      kaug_ref, vt_ref, acc_ref, p_ref, *, tq, tk, unroll, prep_rows, lam_init):
    hd = pl.program_id(1)
    qi = pl.program_id(2)
    s_len = k_ref.shape[1]
    n_terms = len(LOG2E_TERMS)

    @pl.when(qi == 0)
    def _():
        slope_bits = (127 - (8 // N_ATTN_HEADS) * (hd + 1)) << 23
        slope = lax.bitcast_convert_type(jnp.full((prep_rows, LANES), slope_bits, jnp.int32), F32)
        lane = lax.broadcasted_iota(jnp.int32, (prep_rows, LANES), 1)
        rowi = lax.broadcasted_iota(jnp.int32, (prep_rows, LANES), 0)

        def prep(t, _):
            start = pl.multiple_of(t * prep_rows, prep_rows)
            pos = start + rowi
            lo = pos & (POS_SPLIT - 1)
            hi_part = slope * (pos - lo).astype(F32)
            lo_part = slope * lo.astype(F32)
            cols = jnp.where(lane < n_terms, hi_part, jnp.where(lane < 2 * n_terms, lo_part, 0.0))
            kaug_ref[pl.ds(start, prep_rows), 0:LANES] = k_ref[0, pl.ds(start, prep_rows), :]
            kaug_ref[pl.ds(start, prep_rows), LANES:2 * LANES] = cols.astype(BF16)
            vt = v_ref[0, pl.ds(start, prep_rows), :].astype(F32).T
            vt_ref[0:LANES, pl.ds(start, prep_rows)] = vt.astype(BF16)
            return 0

        lax.fori_loop(0, s_len // prep_rows, prep, 0)
        pad_row = lax.broadcasted_iota(jnp.int32, (ACC_ROWS - LANES, s_len), 0)
        vt_ref[LANES:ACC_ROWS, :] = jnp.where(pad_row == 0, 1.0, 0.0).astype(BF16)

    q = q_ref[0]
    lane = lax.broadcasted_iota(jnp.int32, q.shape, 1)
    consts = jnp.zeros(q.shape, F32)
    for t, term in enumerate(LOG2E_TERMS):
        consts = jnp.where((lane == t) | (lane == n_terms + t), term, consts)
    consts = consts.astype(BF16)
    zero = jnp.zeros_like(q)
    qaug = (jnp.concatenate([jnp.where(lane < ATTN_HEAD_DIM, q, zero), consts], axis=1),
            jnp.concatenate([jnp.where(lane >= ATTN_HEAD_DIM, q, zero), consts], axis=1))

    key = lax.broadcasted_iota(jnp.int32, (tk, tq), 0)
    qry = lax.broadcasted_iota(jnp.int32, (tk, tq), 1)
    def scores(j):
        kb = kaug_ref[pl.ds(pl.multiple_of(j * tk, tk), tk), :]
        return [_dot_nt(kb, qaug[c]) for c in range(2)]

    def softmax_block(ss, ms, masked, stale):
        new, coefs, ps, excess = [], [], [], []
        for c in range(2):
            s = ss[c]
            if masked:
                s = jnp.where(key <= qry, s, NEG_BIG)
            block_max = jnp.max(s, axis=0, keepdims=True)
            m_new = jnp.maximum(ms[c], block_max)
            ps.append(jnp.exp2(s - (ms[c] if stale else m_new)).astype(BF16))
            rescale = jnp.exp2(ms[c] - m_new)
            coefs.append((rescale, rescale if stale else jnp.ones_like(rescale)))
            new.append(m_new)
            excess.append(block_max - ms[c])
        return tuple(new), tuple(coefs), ps, jnp.maximum(excess[0], excess[1])

    def accumulate(j, coefs, ps):
        vb = vt_ref[:, pl.ds(pl.multiple_of(j * tk, tk), tk)]
        for c in range(2):
            a, b = coefs[c]
            acc_ref[c] = a * acc_ref[c] + b * _dot(vb, ps[c])

    def attend(stale, blocks_per_trip):
        def trip(n_blocks, stale, first=0):
            def body(t, carry):
                ms, coefs, worst = carry
                ps = [p_ref[0], p_ref[1]]
                for u in range(n_blocks):
                    j = first + t * n_blocks + u
                    ss = scores(j)
                    accumulate(jnp.where(j == 0, qi, j - 1), coefs, ps)
                    ms, coefs, ps, excess = softmax_block(ss, ms, False, stale)
                    if stale:
                        worst = jnp.maximum(worst, excess)
                p_ref[0], p_ref[1] = ps
                return ms, coefs, worst
            return body

        acc_ref[...] = jnp.zeros_like(acc_ref)
        m0 = jnp.full((1, tq), NEG_BIG, F32)
        ms, coefs, ps, _ = softmax_block(scores(qi), (m0, m0), True, False)
        p_ref[0], p_ref[1] = ps
        n_trips = qi // blocks_per_trip
        carry = lax.fori_loop(0, n_trips, trip(blocks_per_trip, stale), (ms, coefs, m0))
        done = n_trips * blocks_per_trip
        if blocks_per_trip > 2:
            n_pairs = (qi - done) // 2
            carry = lax.fori_loop(0, n_pairs, trip(2, stale, done), carry)
            done = done + 2 * n_pairs
        _, coefs, worst = lax.fori_loop(done, qi, trip(1, False), carry)
        accumulate(jnp.where(qi == 0, 0, qi - 1), coefs, [p_ref[0], p_ref[1]])
        return worst

    worst = attend(True, unroll)

    @pl.when(jnp.max(worst) > MAX_STALE_EXCESS)
    def _():
        attend(False, 1)

    lam = (jnp.exp(jnp.sum(lq1_ref[...] * lk1_ref[...], axis=-1, keepdims=True))
           - jnp.exp(jnp.sum(lq2_ref[...] * lk2_ref[...], axis=-1, keepdims=True)) + lam_init)
    a1, a2 = acc_ref[0], acc_ref[1]
    o_t = a1[0:LANES] / a1[LANES:LANES + 1] - lam * (a2[0:LANES] / a2[LANES:LANES + 1])
    o_ref[0] = (_rms(o_t.T, on_ref[...]) * (1.0 - lam_init)).astype(BF16)


def _diff_attn(q, k, v, lq1, lk1, lq2, lk2, onorm, *, tq, tk, lam_init):
    b, s, _ = q.shape
    assert tq == tk, "the causal mask assumes square score blocks"
    qspec = pl.BlockSpec((1, tq, LANES), lambda bi, hi, qi: (bi, qi, hi))
    kvspec = pl.BlockSpec((1, s, LANES), lambda bi, hi, qi: (bi, 0, hi))
    small = lambda w: pl.BlockSpec((1, w), lambda bi, hi, qi: (0, 0))
    return pl.pallas_call(
        functools.partial(_diff_attn_kernel, tq=tq, tk=tk, unroll=KEY_BLOCKS_PER_TRIP,
                          prep_rows=min(s, 512), lam_init=lam_init),
        grid=(b, N_ATTN_HEADS, s // tq),
        in_specs=[qspec, kvspec, kvspec, small(ATTN_HEAD_DIM), small(ATTN_HEAD_DIM),
                  small(ATTN_HEAD_DIM), small(ATTN_HEAD_DIM), small(LANES)],
        out_specs=qspec,
        out_shape=jax.ShapeDtypeStruct((b, s, ATTN_WIDTH), BF16),
        scratch_shapes=[pltpu.VMEM((s, 2 * LANES), BF16),
                        pltpu.VMEM((ACC_ROWS, s), BF16),
                        pltpu.VMEM((2, ACC_ROWS, tq), F32),
                        pltpu.VMEM((2, tk, tq), BF16)],
        compiler_params=pltpu.CompilerParams(
            dimension_semantics=("arbitrary", "arbitrary", "arbitrary"),
            vmem_limit_bytes=VMEM_LIMIT_BYTES),
        name="diff_attn",
    )(q, k, v, lq1, lk1, lq2, lk2, onorm)


def _split3(x):
    hi = x.astype(BF16)
    r = x - hi.astype(F32)
    mid = r.astype(BF16)
    lo = (r - mid.astype(F32)).astype(BF16)
    return hi, mid, lo


def _expand_heads(cols):
    rows = cols.shape[0]
    lane = lax.broadcasted_iota(jnp.int32, (rows, LANES), 1)
    parts = []
    for pr in range(N_SSM_HEADS // 2):
        a = jnp.broadcast_to(cols[:, 2 * pr:2 * pr + 1], (rows, LANES))
        b = jnp.broadcast_to(cols[:, 2 * pr + 1:2 * pr + 2], (rows, LANES))
        parts.append(jnp.where(lane < SSM_HEAD_DIM, a, b))
    return jnp.concatenate(parts, axis=1)


def _ssd_kernel(xbc_ref, z_ref, dt_ref, cw_ref, cb_ref, alog_ref, dskip_ref, on_ref, y_ref,
                xpad_ref, h_ref):
    c = pl.program_id(1)
    L = CHUNK
    halo = SUBLANES

    @pl.when(c == 0)
    def _():
        xpad_ref[0:halo, :] = jnp.zeros((halo, CONV_DIM), F32)
        h_ref[...] = jnp.zeros_like(h_ref)

    cur = xbc_ref[0]
    xpad_ref[halo:halo + L, :] = cur
    conv = cur * cw_ref[CONV_WIDTH - 1:CONV_WIDTH, :] + cb_ref[...]
    for j in range(CONV_WIDTH - 1):
        off = halo - (CONV_WIDTH - 1) + j
        conv = conv + xpad_ref[off:off + L, :] * cw_ref[j:j + 1, :]
    xpad_ref[0:halo, :] = cur[L - halo:, :]
    act = _silu(conv)
    xs = act[:, :SSM_WIDTH]

    dt = jax.nn.softplus(dt_ref[0])
    a = -jnp.exp(alog_ref[...])
    ad = dt * a

    r = lax.broadcasted_iota(jnp.int32, (L, L), 0)
    cc = lax.broadcasted_iota(jnp.int32, (L, L), 1)
    tril = r >= cc
    tri = jnp.where(tril, 1.0, 0.0).astype(BF16)
    hi, mid, lo = _split3(ad)
    a_cs = _dot(tri, hi) + _dot(tri, mid) + _dot(tri, lo)
    a_cs_t = a_cs.T

    dt_l = _expand_heads(dt)
    acs_l = _expand_heads(a_cs)
    last_l = acs_l[L - 1:L, :]
    xd = xs * dt_l
    xw = (xd * jnp.exp(last_l - acs_l)).astype(BF16)
    e_in = jnp.exp(acs_l)
    e_chunk_t = jnp.exp(a_cs_t[:, L - 1:L])
    xd16 = xd.astype(BF16)
    glane = lax.broadcasted_iota(jnp.int32, (L, GROUP_WIDTH), 1)

    ys = []
    for g in range(SSM_GROUPS):
        bg = act[:, SSM_WIDTH + g * SSM_STATE:SSM_WIDTH + (g + 1) * SSM_STATE].astype(BF16)
        cg = act[:, SSM_WIDTH + (SSM_GROUPS + g) * SSM_STATE:
                 SSM_WIDTH + (SSM_GROUPS + g + 1) * SSM_STATE].astype(BF16)
        gs = slice(g * GROUP_WIDTH, (g + 1) * GROUP_WIDTH)
        cb = _dot_nt(cg, bg)
        xd_g = xd16[:, gs]
        y_g = None
        for hl in range(HEADS_PER_GROUP):
            hd = g * HEADS_PER_GROUP + hl
            seg = a_cs[:, hd:hd + 1] - a_cs_t[hd:hd + 1, :]
            m = (cb * jnp.exp(jnp.where(tril, seg, NEG_BIG))).astype(BF16)
            in_head = (glane >= hl * SSM_HEAD_DIM) & (glane < (hl + 1) * SSM_HEAD_DIM)
            d = _dot(m, jnp.where(in_head, xd_g, jnp.zeros_like(xd_g)))
            y_g = d if y_g is None else y_g + d
        h_in = h_ref[g]
        y_off = _dot_nt(cg, h_in.astype(BF16)) * e_in[:, gs]
        ys.append(y_g + y_off)
        decay_rows = jnp.concatenate(
            [jnp.broadcast_to(e_chunk_t[g * HEADS_PER_GROUP + hl:g * HEADS_PER_GROUP + hl + 1, :],
                              (SSM_HEAD_DIM, SSM_STATE)) for hl in range(HEADS_PER_GROUP)], axis=0)
        h_ref[g] = h_in * decay_rows + _dot_tn(xw[:, gs], bg)

    y = jnp.concatenate(ys, axis=1) + dskip_ref[...] * xs
    y = y * _silu(z_ref[0])
    outs = []
    for g in range(SSM_GROUPS):
        gs = slice(g * GROUP_WIDTH, (g + 1) * GROUP_WIDTH)
        outs.append(_rms(y[:, gs], on_ref[:, gs]))
    y_ref[0] = jnp.concatenate(outs, axis=1).astype(BF16)


def _ssd(xbc, z, dt, cw, cb, alog, dskip, onorm):
    b, s, _ = xbc.shape
    blk = lambda w: pl.BlockSpec((1, CHUNK, w), lambda bi, ci: (bi, ci, 0))
    small = lambda r, w: pl.BlockSpec((r, w), lambda bi, ci: (0, 0))
    return pl.pallas_call(
        _ssd_kernel,
        grid=(b, s // CHUNK),
        in_specs=[blk(CONV_DIM), blk(SSM_WIDTH), blk(LANES), small(CONV_WIDTH, CONV_DIM),
                  small(1, CONV_DIM), small(1, LANES), small(1, SSM_WIDTH), small(1, SSM_WIDTH)],
        out_specs=blk(SSM_WIDTH),
        out_shape=jax.ShapeDtypeStruct((b, s, SSM_WIDTH), BF16),
        scratch_shapes=[pltpu.VMEM((SUBLANES + CHUNK, CONV_DIM), F32),
                        pltpu.VMEM((SSM_GROUPS, GROUP_WIDTH, SSM_STATE), F32)],
        compiler_params=pltpu.CompilerParams(dimension_semantics=("arbitrary", "arbitrary"),
                                             vmem_limit_bytes=VMEM_LIMIT_BYTES),
        name="ssd",
    )(xbc, z, dt, cw, cb, alog, dskip, onorm)


def _out_ffn_ple_kernel(x1_ref, oa_ref, ys_ref, p_ref, woa_ref, woy_ref, n2_ref, wg_ref, wu_ref,
                        wd_ref, ng_ref, wpg_ref, wpp_ref, npl_ref, out_ref, *, ff_chunk):
    x2 = x1_ref[...] + _dot(oa_ref[...], woa_ref[...]) + _dot(ys_ref[...], woy_ref[...])
    x3 = _swiglu_residual(x2, n2_ref[...], wg_ref, wu_ref, wd_ref, ff_chunk)
    e = _rms(_dot(p_ref[...].astype(BF16), wpp_ref[...]), npl_ref[...])
    gate = jax.nn.sigmoid(_dot(_rms(x3, ng_ref[...]).astype(BF16), wpg_ref[...]))
    out_ref[...] = x3 + gate * e


def _out_ffn_ple(x1, oa, ys, p, woa, woy, n2, wg, wu, wd, ng, wpg, wpp, npl, *, tm, ff_chunk):
    n, d = x1.shape
    d_ff = wg.shape[1]
    row = lambda w: pl.BlockSpec((tm, w), lambda i: (i, 0))
    return pl.pallas_call(
        functools.partial(_out_ffn_ple_kernel, ff_chunk=ff_chunk),
        grid=(n // tm,),
        in_specs=[row(d), row(ATTN_WIDTH), row(SSM_WIDTH), row(p.shape[1]),
                  _resident(woa.shape), _resident(woy.shape), _resident((1, d)),
                  _resident((d, d_ff)), _resident((d, d_ff)), _resident((d_ff, d)),
                  _resident((1, d)), _resident(wpg.shape), _resident(wpp.shape), _resident((1, d))],
        out_specs=row(d),
        out_shape=jax.ShapeDtypeStruct((n, d), F32),
        compiler_params=pltpu.CompilerParams(dimension_semantics=("arbitrary",),
                                             vmem_limit_bytes=VMEM_LIMIT_BYTES),
        name="out_ffn_ple",
    )(x1, oa, ys, p, woa, woy, n2, wg, wu, wd, ng, wpg, wpp, npl)


def _tiles(n, s, d_ff):
    tm = 512 if n % 512 == 0 else SUBLANES
    tq = 512 if s % 512 == 0 else CHUNK
    tk = tq
    ff_chunk = 6 * MXU_WIDTH
    return tm, tq, tk, ff_chunk


def _layer(i, x, p_i, prm):
    b, s, d = x.shape
    n = b * s
    d_ff = prm["ffn1_w_gate"].shape[-1]
    tm, tq, tk, ff_chunk = _tiles(n, s, d_ff)
    lam_init = 0.8 - 0.6 * math.exp(-0.3 * i)
    row = lambda v: v.reshape(1, -1).astype(F32)
    w16 = lambda w: w.astype(BF16)

    w_in = prm["w_in"]
    n_main = 3 * ATTN_WIDTH + SSM_WIDTH + CONV_DIM
    win = w16(w_in[:, :n_main])
    wdt = w16(jnp.pad(w_in[:, n_main:], ((0, 0), (0, LANES - N_SSM_HEADS))))
    dtb = jnp.pad(row(prm["dt_bias"]), ((0, 0), (0, LANES - N_SSM_HEADS)))
    alog = jnp.pad(row(prm["a_log"]), ((0, 0), (0, LANES - N_SSM_HEADS)))
    qg = jnp.tile(row(prm["q_norm"]), (1, 2)) * (ATTN_HEAD_DIM ** -0.5 * LOG2E)
    kg = jnp.tile(row(prm["k_norm"]), (1, 2))
    dskip = jnp.repeat(row(prm["d_skip"]), SSM_HEAD_DIM, axis=1)
    w_out = prm["w_out"]

    x1, q, k, v, z, xbc, dt = _ffn_inproj(
        x.reshape(n, d), row(prm["ffn1_norm"]), w16(prm["ffn1_w_gate"]), w16(prm["ffn1_w_up"]),
        w16(prm["ffn1_w_down"]), row(prm["mix_norm"]), win, wdt, qg, kg, dtb,
        tm=tm, ff_chunk=ff_chunk)

    r3 = lambda t: t.reshape(b, s, t.shape[-1])
    oa = _diff_attn(r3(q), r3(k), r3(v), row(prm["lambda_q1"]), row(prm["lambda_k1"]),
                    row(prm["lambda_q2"]), row(prm["lambda_k2"]), row(prm["attn_out_norm"]),
                    tq=tq, tk=tk, lam_init=lam_init)
    ys = _ssd(r3(xbc), r3(z), r3(dt), prm["conv_w"].astype(F32), row(prm["conv_b"]), alog, dskip,
              row(prm["ssm_out_norm"]))

    out = _out_ffn_ple(
        x1, oa.reshape(n, ATTN_WIDTH), ys.reshape(n, SSM_WIDTH), p_i.reshape(n, p_i.shape[-1]),
        w16(w_out[:ATTN_WIDTH]), w16(w_out[ATTN_WIDTH:]), row(prm["ffn2_norm"]),
        w16(prm["ffn2_w_gate"]), w16(prm["ffn2_w_up"]), w16(prm["ffn2_w_down"]),
        row(prm["ple_gate_norm"]), w16(prm["w_ple_gate"]), w16(prm["w_ple_proj"]),
        row(prm["ple_norm"]), tm=tm, ff_chunk=ff_chunk)
    return out.reshape(b, s, d)


def kernel(x, p, ffn1_norm, ffn1_w_gate, ffn1_w_up, ffn1_w_down, mix_norm, w_in, q_norm, k_norm, lambda_q1, lambda_k1, lambda_q2, lambda_k2, attn_out_norm, conv_w, conv_b, dt_bias, a_log, d_skip, ssm_out_norm, w_out, ffn2_norm, ffn2_w_gate, ffn2_w_up, ffn2_w_down, ple_gate_norm, w_ple_gate, w_ple_proj, ple_norm):
    stacked = dict(
        ffn1_norm=ffn1_norm, ffn1_w_gate=ffn1_w_gate, ffn1_w_up=ffn1_w_up, ffn1_w_down=ffn1_w_down,
        mix_norm=mix_norm, w_in=w_in, q_norm=q_norm, k_norm=k_norm, lambda_q1=lambda_q1,
        lambda_k1=lambda_k1, lambda_q2=lambda_q2, lambda_k2=lambda_k2, attn_out_norm=attn_out_norm,
        conv_w=conv_w, conv_b=conv_b, dt_bias=dt_bias, a_log=a_log, d_skip=d_skip,
        ssm_out_norm=ssm_out_norm, w_out=w_out, ffn2_norm=ffn2_norm, ffn2_w_gate=ffn2_w_gate,
        ffn2_w_up=ffn2_w_up, ffn2_w_down=ffn2_w_down, ple_gate_norm=ple_gate_norm,
        w_ple_gate=w_ple_gate, w_ple_proj=w_ple_proj, ple_norm=ple_norm)
    for i in range(p.shape[0]):
        x = _layer(i, x, p[i], {name: w[i] for name, w in stacked.items()})
    return x
```

```python
import functools
import math

import numpy as np
import jax
import jax.numpy as jnp
from jax import lax
from jax.experimental import pallas as pl
from jax.experimental.pallas import tpu as pltpu

N_ATTN_HEADS = 4
ATTN_HEAD_DIM = 64
SSM_HEAD_DIM = 64
N_SSM_HEADS = 8
SSM_GROUPS = 2
SSM_STATE = 128
CONV_WIDTH = 4
CHUNK = 256
NORM_EPS = 1e-6

LANES = 128
SUBLANES = 8
MXU_WIDTH = 256
ROW_PARTS = 2
VMEM_LIMIT_BYTES = 56 * 1024 * 1024

ATTN_WIDTH = 2 * N_ATTN_HEADS * ATTN_HEAD_DIM
SSM_WIDTH = N_SSM_HEADS * SSM_HEAD_DIM
HEADS_PER_GROUP = N_SSM_HEADS // SSM_GROUPS
GROUP_WIDTH = HEADS_PER_GROUP * SSM_HEAD_DIM
CONV_DIM = SSM_WIDTH + 2 * SSM_GROUPS * SSM_STATE
NEG_BIG = -1e30

BF16 = jnp.bfloat16
F32 = jnp.float32


def _rms(x, g):
    ms = jnp.mean(x * x, axis=-1, keepdims=True)
    return x * lax.rsqrt(ms + NORM_EPS) * g


def _silu(x):
    return x * jax.nn.sigmoid(x)


def _dot(a, b):
    return jnp.dot(a, b, preferred_element_type=F32)


def _dot_nt(a, b):
    return lax.dot_general(a, b, (((1,), (1,)), ((), ())), preferred_element_type=F32)


def _dot_tn(a, b):
    return lax.dot_general(a, b, (((0,), (0,)), ((), ())), preferred_element_type=F32)


def _resident(shape):
    zeros = (0,) * len(shape)
    return pl.BlockSpec(shape, lambda *_: zeros, pipeline_mode=pl.Buffered(1))


def _swiglu_residual(x, norm, wg_ref, wu_ref, wd_ref, ff_chunk):
    h = _rms(x, norm).astype(BF16)
    d_ff = wg_ref.shape[1]
    acc = None
    for lo in range(0, d_ff, ff_chunk):
        hi = min(lo + ff_chunk, d_ff)
        g = _dot(h, wg_ref[:, lo:hi])
        u = _dot(h, wu_ref[:, lo:hi])
        a = (_silu(g) * u).astype(BF16)
        d = _dot(a, wd_ref[lo:hi, :])
        acc = d if acc is None else acc + d
    return x + 0.5 * acc


def _pair_rms(x, g):
    lane = lax.broadcasted_iota(jnp.int32, x.shape, 1)
    lo = lane < ATTN_HEAD_DIM
    sq = x * x
    s_all = jnp.sum(sq, axis=-1, keepdims=True)
    s_lo = jnp.sum(jnp.where(lo, sq, 0.0), axis=-1, keepdims=True)
    s_hi = s_all - s_lo
    inv = 1.0 / ATTN_HEAD_DIM
    r = jnp.where(lo, lax.rsqrt(s_lo * inv + NORM_EPS), lax.rsqrt(s_hi * inv + NORM_EPS))
    return x * r * g


def _ffn_inproj_kernel(x_ref, n1_ref, wg_ref, wu_ref, wd_ref, nm_ref, win_ref, wdt_ref,
                       qg_ref, kg_ref, dtb_ref,
                       x1_ref, q_ref, k_ref, v_ref, z_ref, xbc_ref, dt_ref, *, ff_chunk):
    aw = ATTN_WIDTH
    part = x_ref.shape[0] // ROW_PARTS
    for r in (slice(i * part, (i + 1) * part) for i in range(ROW_PARTS)):
        x1 = _swiglu_residual(x_ref[r, :], n1_ref[...], wg_ref, wu_ref, wd_ref, ff_chunk)
        x1_ref[r, :] = x1
        h = _rms(x1, nm_ref[...]).astype(BF16)
        q = _dot(h, win_ref[:, 0:aw])
        k = _dot(h, win_ref[:, aw:2 * aw])
        for hd in range(N_ATTN_HEADS):
            sl = slice(hd * LANES, (hd + 1) * LANES)
            q_ref[r, sl] = _pair_rms(q[:, sl], qg_ref[...]).astype(BF16)
            k_ref[r, sl] = _pair_rms(k[:, sl], kg_ref[...]).astype(BF16)
        v_ref[r, :] = _dot(h, win_ref[:, 2 * aw:3 * aw]).astype(BF16)
        z_ref[r, :] = _dot(h, win_ref[:, 3 * aw:3 * aw + SSM_WIDTH])
        xbc_ref[r, :] = _dot(h, win_ref[:, 3 * aw + SSM_WIDTH:])
        dt_ref[r, :] = _dot(h, wdt_ref[...]) + dtb_ref[...]


def _ffn_inproj(x, n1, wg, wu, wd, nm, win, wdt, qg, kg, dtb, *, tm, ff_chunk):
    n, d = x.shape
    d_ff = wg.shape[1]
    row = lambda w: pl.BlockSpec((tm, w), lambda i: (i, 0))
    out_shape = (
        jax.ShapeDtypeStruct((n, d), F32),
        jax.ShapeDtypeStruct((n, ATTN_WIDTH), BF16),
        jax.ShapeDtypeStruct((n, ATTN_WIDTH), BF16),
        jax.ShapeDtypeStruct((n, ATTN_WIDTH), BF16),
        jax.ShapeDtypeStruct((n, SSM_WIDTH), F32),
        jax.ShapeDtypeStruct((n, CONV_DIM), F32),
        jax.ShapeDtypeStruct((n, LANES), F32),
    )
    return pl.pallas_call(
        functools.partial(_ffn_inproj_kernel, ff_chunk=ff_chunk),
        grid=(n // tm,),
        in_specs=[row(d), _resident((1, d)), _resident((d, d_ff)), _resident((d, d_ff)),
                  _resident((d_ff, d)), _resident((1, d)), _resident(win.shape), _resident(wdt.shape),
                  _resident((1, LANES)), _resident((1, LANES)), _resident((1, LANES))],
        out_specs=(row(d), row(ATTN_WIDTH), row(ATTN_WIDTH), row(ATTN_WIDTH), row(SSM_WIDTH),
                   row(CONV_DIM), row(LANES)),
        out_shape=out_shape,
        compiler_params=pltpu.CompilerParams(dimension_semantics=("arbitrary",),
                                             vmem_limit_bytes=VMEM_LIMIT_BYTES),
        name="ffn_inproj",
    )(x, n1, wg, wu, wd, nm, win, wdt, qg, kg, dtb)


def _bf16_terms(x, n):
    terms = []
    for _ in range(n):
        t = float(np.asarray(x, dtype=BF16))
        terms.append(t)
        x -= t
    return terms


LOG2E = math.log2(math.e)
LOG2E_TERMS = _bf16_terms(LOG2E, 3)
POS_SPLIT = 128
ACC_ROWS = LANES + 16
KEY_BLOCKS_PER_TRIP = 4
MAX_STALE_EXCESS = 64.0


def _diff_attn_kernel(q_ref, k_ref, v_ref, lq1_ref, lk1_ref, lq2_ref, lk2_ref, on_ref, o_ref,
                      kaug_ref, vt_ref, acc_ref, p_ref, *, tq, tk, unroll, prep_rows, lam_init):
    hd = pl.program_id(1)
    qi = pl.program_id(2)
    s_len = k_ref.shape[1]
    n_terms = len(LOG2E_TERMS)

    @pl.when(qi == 0)
    def _():
        slope_bits = (127 - (8 // N_ATTN_HEADS) * (hd + 1)) << 23
        slope = lax.bitcast_convert_type(jnp.full((prep_rows, LANES), slope_bits, jnp.int32), F32)
        lane = lax.broadcasted_iota(jnp.int32, (prep_rows, LANES), 1)
        rowi = lax.broadcasted_iota(jnp.int32, (prep_rows, LANES), 0)

        def prep(t, _):
            start = pl.multiple_of(t * prep_rows, prep_rows)
            pos = start + rowi
            lo = pos & (POS_SPLIT - 1)
            hi_part = slope * (pos - lo).astype(F32)
            lo_part = slope * lo.astype(F32)
            cols = jnp.where(lane < n_terms, hi_part, jnp.where(lane < 2 * n_terms, lo_part, 0.0))
            kaug_ref[pl.ds(start, prep_rows), 0:LANES] = k_ref[0, pl.ds(start, prep_rows), :]
            kaug_ref[pl.ds(start, prep_rows), LANES:2 * LANES] = cols.astype(BF16)
            vt = v_ref[0, pl.ds(start, prep_rows), :].astype(F32).T
            vt_ref[0:LANES, pl.ds(start, prep_rows)] = vt.astype(BF16)
            return 0

        lax.fori_loop(0, s_len // prep_rows, prep, 0)
        pad_row = lax.broadcasted_iota(jnp.int32, (ACC_ROWS - LANES, s_len), 0)
        vt_ref[LANES:ACC_ROWS, :] = jnp.where(pad_row == 0, 1.0, 0.0).astype(BF16)

    q = q_ref[0]
    lane = lax.broadcasted_iota(jnp.int32, q.shape, 1)
    consts = jnp.zeros(q.shape, F32)
    for t, term in enumerate(LOG2E_TERMS):
        consts = jnp.where((lane == t) | (lane == n_terms + t), term, consts)
    consts = consts.astype(BF16)
    zero = jnp.zeros_like(q)
    qaug = (jnp.concatenate([jnp.where(lane < ATTN_HEAD_DIM, q, zero), consts], axis=1),
            jnp.concatenate([jnp.where(lane >= ATTN_HEAD_DIM, q, zero), consts], axis=1))
    qaug_t = [qa.astype(F32).T.astype(BF16) for qa in qaug]

    key = lax.broadcasted_iota(jnp.int32, (tk, tq), 0)
    qry = lax.broadcasted_iota(jnp.int32, (tk, tq), 1)

    def scores(j):
        kb = kaug_ref[pl.ds(pl.multiple_of(j * tk, tk), tk), :]
        return [_dot(kb, qaug_t[c]) for c in range(2)]

    def softmax_block(ss, ms, masked, stale):
        new, coefs, ps, excess = [], [], [], []
        for c in range(2):
            s = ss[c]
            if masked:
                s = jnp.where(key <= qry, s, NEG_BIG)
            block_max = jnp.max(s, axis=0, keepdims=True)
            m_new = jnp.maximum(ms[c], block_max)
            ps.append(jnp.exp2(s - (ms[c] if stale else m_new)).astype(BF16))
            rescale = jnp.exp2(ms[c] - m_new)
            coefs.append((rescale, rescale if stale else jnp.ones_like(rescale)))
            new.append(m_new)
            excess.append(block_max - ms[c])
        return tuple(new), tuple(coefs), ps, jnp.maximum(excess[0], excess[1])

    def accumulate(j, coefs, ps):
        vb = vt_ref[:, pl.ds(pl.multiple_of(j * tk, tk), tk)]
        for c in range(2):
            a, b = coefs[c]
            acc_ref[c] = a * acc_ref[c] + b * _dot(vb, ps[c])

    def attend(stale, blocks_per_trip):
        def trip(n_blocks, stale, first=0):
            def body(t, carry):
                ms, coefs, worst = carry
                ps = [p_ref[0], p_ref[1]]
                for u in range(n_blocks):
                    j = first + t * n_blocks + u
                    ss = scores(j)
                    accumulate(jnp.where(j == 0, qi, j - 1), coefs, ps)
                    ms, coefs, ps, excess = softmax_block(ss, ms, False, stale)
                    if stale:
                        worst = jnp.maximum(worst, excess)
                p_ref[0], p_ref[1] = ps
                return ms, coefs, worst
            return body

        acc_ref[...] = jnp.zeros_like(acc_ref)
        m0 = jnp.full((1, tq), NEG_BIG, F32)
        ms, coefs, ps, _ = softmax_block(scores(qi), (m0, m0), True, False)
        p_ref[0], p_ref[1] = ps
        n_trips = qi // blocks_per_trip
        carry = lax.fori_loop(0, n_trips, trip(blocks_per_trip, stale), (ms, coefs, m0))
        done = n_trips * blocks_per_trip
        if blocks_per_trip > 2:
            n_pairs = (qi - done) // 2
            carry = lax.fori_loop(0, n_pairs, trip(2, stale, done), carry)
            done = done + 2 * n_pairs
        _, coefs, worst = lax.fori_loop(done, qi, trip(1, False), carry)
        accumulate(jnp.where(qi == 0, 0, qi - 1), coefs, [p_ref[0], p_ref[1]])
        return worst

    worst = attend(True, unroll)

    @pl.when(jnp.max(worst) > MAX_STALE_EXCESS)
    def _():
        attend(False, 1)

    lam = (jnp.exp(jnp.sum(lq1_ref[...] * lk1_ref[...], axis=-1, keepdims=True))
           - jnp.exp(jnp.sum(lq2_ref[...] * lk2_ref[...], axis=-1, keepdims=True)) + lam_init)
    a1, a2 = acc_ref[0], acc_ref[1]
    o_t = a1[0:LANES] / a1[LANES:LANES + 1] - lam * (a2[0:LANES] / a2[LANES:LANES + 1])
    o_ref[0] = (_rms(o_t.T, on_ref[...]) * (1.0 - lam_init)).astype(BF16)


def _diff_attn(q, k, v, lq1, lk1, lq2, lk2, onorm, *, tq, tk, lam_init):
    b, s, _ = q.shape
    assert tq == tk, "the causal mask assumes square score blocks"
    qspec = pl.BlockSpec((1, tq, LANES), lambda bi, hi, qi: (bi, qi, hi))
    kvspec = pl.BlockSpec((1, s, LANES), lambda bi, hi, qi: (bi, 0, hi))
    small = lambda w: pl.BlockSpec((1, w), lambda bi, hi, qi: (0, 0))
    return pl.pallas_call(
        functools.partial(_diff_attn_kernel, tq=tq, tk=tk, unroll=KEY_BLOCKS_PER_TRIP,
                          prep_rows=min(s, 512), lam_init=lam_init),
        grid=(b, N_ATTN_HEADS, s // tq),
        in_specs=[qspec, kvspec, kvspec, small(ATTN_HEAD_DIM), small(ATTN_HEAD_DIM),
                  small(ATTN_HEAD_DIM), small(ATTN_HEAD_DIM), small(LANES)],
        out_specs=qspec,
        out_shape=jax.ShapeDtypeStruct((b, s, ATTN_WIDTH), BF16),
        scratch_shapes=[pltpu.VMEM((s, 2 * LANES), BF16),
                        pltpu.VMEM((ACC_ROWS, s), BF16),
                        pltpu.VMEM((2, ACC_ROWS, tq), F32),
                        pltpu.VMEM((2, tk, tq), BF16)],
        compiler_params=pltpu.CompilerParams(
            dimension_semantics=("arbitrary", "arbitrary", "arbitrary"),
            vmem_limit_bytes=VMEM_LIMIT_BYTES),
        name="diff_attn",
    )(q, k, v, lq1, lk1, lq2, lk2, onorm)


def _split3(x):
    hi = x.astype(BF16)
    r = x - hi.astype(F32)
    mid = r.astype(BF16)
    lo = (r - mid.astype(F32)).astype(BF16)
    return hi, mid, lo


def _expand_heads(cols):
    rows = cols.shape[0]
    lane = lax.broadcasted_iota(jnp.int32, (rows, LANES), 1)
    parts = []
    for pr in range(N_SSM_HEADS // 2):
        a = jnp.broadcast_to(cols[:, 2 * pr:2 * pr + 1], (rows, LANES))
        b = jnp.broadcast_to(cols[:, 2 * pr + 1:2 * pr + 2], (rows, LANES))
        parts.append(jnp.where(lane < SSM_HEAD_DIM, a, b))
    return jnp.concatenate(parts, axis=1)


def _ssd_kernel(xbc_ref, z_ref, dt_ref, cw_ref, cb_ref, alog_ref, dskip_ref, on_ref, y_ref,
                xpad_ref, h_ref):
    c = pl.program_id(1)
    L = CHUNK
    halo = SUBLANES

    @pl.when(c == 0)
    def _():
        xpad_ref[0:halo, :] = jnp.zeros((halo, CONV_DIM), F32)
        h_ref[...] = jnp.zeros_like(h_ref)

    cur = xbc_ref[0]
    xpad_ref[halo:halo + L, :] = cur
    conv = cur * cw_ref[CONV_WIDTH - 1:CONV_WIDTH, :] + cb_ref[...]
    for j in range(CONV_WIDTH - 1):
        off = halo - (CONV_WIDTH - 1) + j
        conv = conv + xpad_ref[off:off + L, :] * cw_ref[j:j + 1, :]
    xpad_ref[0:halo, :] = cur[L - halo:, :]
    act = _silu(conv)
    xs = act[:, :SSM_WIDTH]

    dt = jax.nn.softplus(dt_ref[0])
    a = -jnp.exp(alog_ref[...])
    ad = dt * a

    r = lax.broadcasted_iota(jnp.int32, (L, L), 0)
    cc = lax.broadcasted_iota(jnp.int32, (L, L), 1)
    tril = r >= cc
    tri = jnp.where(tril, 1.0, 0.0).astype(BF16)
    hi, mid, lo = _split3(ad)
    a_cs = _dot(tri, hi) + _dot(tri, mid) + _dot(tri, lo)
    a_cs_t = a_cs.T

    dt_l = _expand_heads(dt)
    acs_l = _expand_heads(a_cs)
    last_l = acs_l[L - 1:L, :]
    xd = xs * dt_l
    xw = (xd * jnp.exp(last_l - acs_l)).astype(BF16)
    e_in = jnp.exp(acs_l)
    e_chunk_t = jnp.exp(a_cs_t[:, L - 1:L])
    xd16 = xd.astype(BF16)
    glane = lax.broadcasted_iota(jnp.int32, (L, GROUP_WIDTH), 1)

    ys = []
    for g in range(SSM_GROUPS):
        bg = act[:, SSM_WIDTH + g * SSM_STATE:SSM_WIDTH + (g + 1) * SSM_STATE].astype(BF16)
        cg = act[:, SSM_WIDTH + (SSM_GROUPS + g) * SSM_STATE:
                 SSM_WIDTH + (SSM_GROUPS + g + 1) * SSM_STATE].astype(BF16)
        gs = slice(g * GROUP_WIDTH, (g + 1) * GROUP_WIDTH)
        cb = _dot_nt(cg, bg)
        xd_g = xd16[:, gs]
        y_g = None
        for hl in range(HEADS_PER_GROUP):
            hd = g * HEADS_PER_GROUP + hl
            seg = a_cs[:, hd:hd + 1] - a_cs_t[hd:hd + 1, :]
            m = (cb * jnp.exp(jnp.where(tril, seg, NEG_BIG))).astype(BF16)
            in_head = (glane >= hl * SSM_HEAD_DIM) & (glane < (hl + 1) * SSM_HEAD_DIM)
            d = _dot(m, jnp.where(in_head, xd_g, jnp.zeros_like(xd_g)))
            y_g = d if y_g is None else y_g + d
        h_in = h_ref[g]
        y_off = _dot_nt(cg, h_in.astype(BF16)) * e_in[:, gs]
        ys.append(y_g + y_off)
        decay_rows = jnp.concatenate(
            [jnp.broadcast_to(e_chunk_t[g * HEADS_PER_GROUP + hl:g * HEADS_PER_GROUP + hl + 1, :],
                              (SSM_HEAD_DIM, SSM_STATE)) for hl in range(HEADS_PER_GROUP)], axis=0)
        h_ref[g] = h_in * decay_rows + _dot_tn(xw[:, gs], bg)

    y = jnp.concatenate(ys, axis=1) + dskip_ref[...] * xs
    y = y * _silu(z_ref[0])
    outs = []
    for g in range(SSM_GROUPS):
        gs = slice(g * GROUP_WIDTH, (g + 1) * GROUP_WIDTH)
        outs.append(_rms(y[:, gs], on_ref[:, gs]))
    y_ref[0] = jnp.concatenate(outs, axis=1).astype(BF16)


def _ssd(xbc, z, dt, cw, cb, alog, dskip, onorm):
    b, s, _ = xbc.shape
    blk = lambda w: pl.BlockSpec((1, CHUNK, w), lambda bi, ci: (bi, ci, 0))
    small = lambda r, w: pl.BlockSpec((r, w), lambda bi, ci: (0, 0))
    return pl.pallas_call(
        _ssd_kernel,
        grid=(b, s // CHUNK),
        in_specs=[blk(CONV_DIM), blk(SSM_WIDTH), blk(LANES), small(CONV_WIDTH, CONV_DIM),
                  small(1, CONV_DIM), small(1, LANES), small(1, SSM_WIDTH), small(1, SSM_WIDTH)],
        out_specs=blk(SSM_WIDTH),
        out_shape=jax.ShapeDtypeStruct((b, s, SSM_WIDTH), BF16),
        scratch_shapes=[pltpu.VMEM((SUBLANES + CHUNK, CONV_DIM), F32),
                        pltpu.VMEM((SSM_GROUPS, GROUP_WIDTH, SSM_STATE), F32)],
        compiler_params=pltpu.CompilerParams(dimension_semantics=("arbitrary", "arbitrary"),
                                             vmem_limit_bytes=VMEM_LIMIT_BYTES),
        name="ssd",
    )(xbc, z, dt, cw, cb, alog, dskip, onorm)


def _out_ffn_ple_kernel(x1_ref, oa_ref, ys_ref, p_ref, woa_ref, woy_ref, n2_ref, wg_ref, wu_ref,
                        wd_ref, ng_ref, wpg_ref, wpp_ref, npl_ref, out_ref, *, ff_chunk):
    x2 = x1_ref[...] + _dot(oa_ref[...], woa_ref[...]) + _dot(ys_ref[...], woy_ref[...])
    x3 = _swiglu_residual(x2, n2_ref[...], wg_ref, wu_ref, wd_ref, ff_chunk)
    e = _rms(_dot(p_ref[...].astype(BF16), wpp_ref[...]), npl_ref[...])
    gate = jax.nn.sigmoid(_dot(_rms(x3, ng_ref[...]).astype(BF16), wpg_ref[...]))
    out_ref[...] = x3 + gate * e


def _out_ffn_ple(x1, oa, ys, p, woa, woy, n2, wg, wu, wd, ng, wpg, wpp, npl, *, tm, ff_chunk):
    n, d = x1.shape
    d_ff = wg.shape[1]
    row = lambda w: pl.BlockSpec((tm, w), lambda i: (i, 0))
    return pl.pallas_call(
        functools.partial(_out_ffn_ple_kernel, ff_chunk=ff_chunk),
        grid=(n // tm,),
        in_specs=[row(d), row(ATTN_WIDTH), row(SSM_WIDTH), row(p.shape[1]),
                  _resident(woa.shape), _resident(woy.shape), _resident((1, d)),
                  _resident((d, d_ff)), _resident((d, d_ff)), _resident((d_ff, d)),
                  _resident((1, d)), _resident(wpg.shape), _resident(wpp.shape), _resident((1, d))],
        out_specs=row(d),
        out_shape=jax.ShapeDtypeStruct((n, d), F32),
        compiler_params=pltpu.CompilerParams(dimension_semantics=("arbitrary",),
                                             vmem_limit_bytes=VMEM_LIMIT_BYTES),
        name="out_ffn_ple",
    )(x1, oa, ys, p, woa, woy, n2, wg, wu, wd, ng, wpg, wpp, npl)


def _tiles(n, s, d_ff):
    tm = 512 if n % 512 == 0 else SUBLANES
    tq = 512 if s % 512 == 0 else CHUNK
    tk = tq
    ff_chunk = 6 * MXU_WIDTH
    return tm, tq, tk, ff_chunk


def _layer(i, x, p_i, prm):
    b, s, d = x.shape
    n = b * s
    d_ff = prm["ffn1_w_gate"].shape[-1]
    tm, tq, tk, ff_chunk = _tiles(n, s, d_ff)
    lam_init = 0.8 - 0.6 * math.exp(-0.3 * i)
    row = lambda v: v.reshape(1, -1).astype(F32)
    w16 = lambda w: w.astype(BF16)

    w_in = prm["w_in"]
    n_main = 3 * ATTN_WIDTH + SSM_WIDTH + CONV_DIM
    win = w16(w_in[:, :n_main])
    wdt = w16(jnp.pad(w_in[:, n_main:], ((0, 0), (0, LANES - N_SSM_HEADS))))
    dtb = jnp.pad(row(prm["dt_bias"]), ((0, 0), (0, LANES - N_SSM_HEADS)))
    alog = jnp.pad(row(prm["a_log"]), ((0, 0), (0, LANES - N_SSM_HEADS)))
    qg = jnp.tile(row(prm["q_norm"]), (1, 2)) * (ATTN_HEAD_DIM ** -0.5 * LOG2E)
    kg = jnp.tile(row(prm["k_norm"]), (1, 2))
    dskip = jnp.repeat(row(prm["d_skip"]), SSM_HEAD_DIM, axis=1)
    w_out = prm["w_out"]

    x1, q, k, v, z, xbc, dt = _ffn_inproj(
        x.reshape(n, d), row(prm["ffn1_norm"]), w16(prm["ffn1_w_gate"]), w16(prm["ffn1_w_up"]),
        w16(prm["ffn1_w_down"]), row(prm["mix_norm"]), win, wdt, qg, kg, dtb,
        tm=tm, ff_chunk=ff_chunk)

    r3 = lambda t: t.reshape(b, s, t.shape[-1])
    oa = _diff_attn(r3(q), r3(k), r3(v), row(prm["lambda_q1"]), row(prm["lambda_k1"]),
                    row(prm["lambda_q2"]), row(prm["lambda_k2"]), row(prm["attn_out_norm"]),
                    tq=tq, tk=tk, lam_init=lam_init)
    ys = _ssd(r3(xbc), r3(z), r3(dt), prm["conv_w"].astype(F32), row(prm["conv_b"]), alog, dskip,
              row(prm["ssm_out_norm"]))

    out = _out_ffn_ple(
        x1, oa.reshape(n, ATTN_WIDTH), ys.reshape(n, SSM_WIDTH), p_i.reshape(n, p_i.shape[-1]),
        w16(w_out[:ATTN_WIDTH]), w16(w_out[ATTN_WIDTH:]), row(prm["ffn2_norm"]),
        w16(prm["ffn2_w_gate"]), w16(prm["ffn2_w_up"]), w16(prm["ffn2_w_down"]),
        row(prm["ple_gate_norm"]), w16(prm["w_ple_gate"]), w16(prm["w_ple_proj"]),
        row(prm["ple_norm"]), tm=tm, ff_chunk=ff_chunk)
    return out.reshape(b, s, d)


def kernel(x, p, ffn1_norm, ffn1_w_gate, ffn1_w_up, ffn1_w_down, mix_norm, w_in, q_norm, k_norm, lambda_q1, lambda_k1, lambda_q2, lambda_k2, attn_out_norm, conv_w, conv_b, dt_bias, a_log, d_skip, ssm_out_norm, w_out, ffn2_norm, ffn2_w_gate, ffn2_w_up, ffn2_w_down, ple_gate_norm, w_ple_gate, w_ple_proj, ple_norm):
    stacked = dict(
        ffn1_norm=ffn1_norm, ffn1_w_gate=ffn1_w_gate, ffn1_w_up=ffn1_w_up, ffn1_w_down=ffn1_w_down,
        mix_norm=mix_norm, w_in=w_in, q_norm=q_norm, k_norm=k_norm, lambda_q1=lambda_q1,
        lambda_k1=lambda_k1, lambda_q2=lambda_q2, lambda_k2=lambda_k2, attn_out_norm=attn_out_norm,
        conv_w=conv_w, conv_b=conv_b, dt_bias=dt_bias, a_log=a_log, d_skip=d_skip,
        ssm_out_norm=ssm_out_norm, w_out=w_out, ffn2_norm=ffn2_norm, ffn2_w_gate=ffn2_w_gate,
        ffn2_w_up=ffn2_w_up, ffn2_w_down=ffn2_w_down, ple_gate_norm=ple_gate_norm,
        w_ple_gate=w_ple_gate, w_ple_proj=w_ple_proj, ple_norm=ple_norm)
    for i in range(p.shape[0]):
        x = _layer(i, x, p[i], {name: w[i] for name, w in stacked.items()})
    return x
```

```python
import functools
import math

import numpy as np
import jax
import jax.numpy as jnp
from jax import lax
from jax.experimental import pallas as pl
from jax.experimental.pallas import tpu as pltpu

N_ATTN_HEADS = 4
ATTN_HEAD_DIM = 64
SSM_HEAD_DIM = 64
N_SSM_HEADS = 8
SSM_GROUPS = 2
SSM_STATE = 128
CONV_WIDTH = 4
CHUNK = 256
NORM_EPS = 1e-6

LANES = 128
SUBLANES = 8
MXU_WIDTH = 256
ROW_PARTS = 2
VMEM_LIMIT_BYTES = 56 * 1024 * 1024

ATTN_WIDTH = 2 * N_ATTN_HEADS * ATTN_HEAD_DIM
SSM_WIDTH = N_SSM_HEADS * SSM_HEAD_DIM
HEADS_PER_GROUP = N_SSM_HEADS // SSM_GROUPS
GROUP_WIDTH = HEADS_PER_GROUP * SSM_HEAD_DIM
CONV_DIM = SSM_WIDTH + 2 * SSM_GROUPS * SSM_STATE
NEG_BIG = -1e30

BF16 = jnp.bfloat16
F32 = jnp.float32


def _bf16_terms(x, n):
    terms = []
    for _ in range(n):
        t = float(np.asarray(x, dtype=BF16))
        terms.append(t)
        x -= t
    return terms


LOG2E = math.log2(math.e)
LOG2E_TERMS = _bf16_terms(LOG2E, 3)
POS_SPLIT = 128
ACC_ROWS = LANES + 16


def _rms(x, g):
    ms = jnp.mean(x * x, axis=-1, keepdims=True)
    return x * lax.rsqrt(ms + NORM_EPS) * g


def _silu(x):
    return x * jax.nn.sigmoid(x)


def _dot(a, b):
    return jnp.dot(a, b, preferred_element_type=F32)


def _dot_nt(a, b):
    return lax.dot_general(a, b, (((1,), (1,)), ((), ())), preferred_element_type=F32)


def _dot_tn(a, b):
    return lax.dot_general(a, b, (((0,), (0,)), ((), ())), preferred_element_type=F32)


def _resident(shape):
    zeros = (0,) * len(shape)
    return pl.BlockSpec(shape, lambda *_: zeros, pipeline_mode=pl.Buffered(1))


def _swiglu_residual(x, norm, wg_ref, wu_ref, wd_ref, ff_chunk):
    h = _rms(x, norm).astype(BF16)
    d_ff = wg_ref.shape[1]
    acc = None
    for lo in range(0, d_ff, ff_chunk):
        hi = min(lo + ff_chunk, d_ff)
        g = _dot(h, wg_ref[:, lo:hi])
        u = _dot(h, wu_ref[:, lo:hi])
        a = (_silu(g) * u).astype(BF16)
        d = _dot(a, wd_ref[lo:hi, :])
        acc = d if acc is None else acc + d
    return x + 0.5 * acc


def _pair_rms(x, g):
    lane = lax.broadcasted_iota(jnp.int32, x.shape, 1)
    lo = lane < ATTN_HEAD_DIM
    sq = x * x
    s_all = jnp.sum(sq, axis=-1, keepdims=True)
    s_lo = jnp.sum(jnp.where(lo, sq, 0.0), axis=-1, keepdims=True)
    s_hi = s_all - s_lo
    inv = 1.0 / ATTN_HEAD_DIM
    r = jnp.where(lo, lax.rsqrt(s_lo * inv + NORM_EPS), lax.rsqrt(s_hi * inv + NORM_EPS))
    return x * r * g


def _ffn_inproj_kernel(x_ref, n1_ref, wg_ref, wu_ref, wd_ref, nm_ref, win_ref, wdt_ref,
                       qg_ref, kg_ref, dtb_ref,
                       x1_ref, q_ref, kaug_ref, vt_ref, z_ref, xbc_ref, dt_ref, *,
                       ff_chunk, tiles_per_seq):
    aw = ATTN_WIDTH
    tm = x_ref.shape[0]
    part = tm // ROW_PARTS
    seq_start = (pl.program_id(0) % tiles_per_seq) * tm
    n_terms = len(LOG2E_TERMS)
    lane = lax.broadcasted_iota(jnp.int32, (part, LANES), 1)
    pad_row = lax.broadcasted_iota(jnp.int32, (ACC_ROWS - LANES, part), 0)
    ones_row = jnp.where(pad_row == 0, 1.0, 0.0).astype(BF16)
    for r in (slice(i * part, (i + 1) * part) for i in range(ROW_PARTS)):
        x1 = _swiglu_residual(x_ref[r, :], n1_ref[...], wg_ref, wu_ref, wd_ref, ff_chunk)
        x1_ref[r, :] = x1
        h = _rms(x1, nm_ref[...]).astype(BF16)
        q = _dot(h, win_ref[:, 0:aw])
        k = _dot(h, win_ref[:, aw:2 * aw])
        v = _dot(h, win_ref[:, 2 * aw:3 * aw])
        pos = seq_start + r.start + lax.broadcasted_iota(jnp.int32, (part, LANES), 0)
        pos_lo = pos & (POS_SPLIT - 1)
        pos_hi = (pos - pos_lo).astype(F32)
        pos_lo = pos_lo.astype(F32)
        for hd in range(N_ATTN_HEADS):
            sl = slice(hd * LANES, (hd + 1) * LANES)
            q_ref[r, sl] = _pair_rms(q[:, sl], qg_ref[...]).astype(BF16)
            slope = 2.0 ** (-8.0 * (hd + 1) / N_ATTN_HEADS)
            cols = jnp.where(lane < n_terms, slope * pos_hi,
                             jnp.where(lane < 2 * n_terms, slope * pos_lo, 0.0))
            kaug_ref[r, 2 * hd * LANES:(2 * hd + 1) * LANES] = (
                _pair_rms(k[:, sl], kg_ref[...]).astype(BF16))
            kaug_ref[r, (2 * hd + 1) * LANES:(2 * hd + 2) * LANES] = cols.astype(BF16)
            vt_ref[0, hd, 0:LANES, r] = v[:, sl].T.astype(BF16)
            vt_ref[0, hd, LANES:ACC_ROWS, r] = ones_row
        z_ref[r, :] = _dot(h, win_ref[:, 3 * aw:3 * aw + SSM_WIDTH])
        xbc_ref[r, :] = _dot(h, win_ref[:, 3 * aw + SSM_WIDTH:])
        dt_ref[r, :] = _dot(h, wdt_ref[...]) + dtb_ref[...]


def _ffn_inproj(x, n1, wg, wu, wd, nm, win, wdt, qg, kg, dtb, *, seq_len, tm, ff_chunk):
    n, d = x.shape
    d_ff = wg.shape[1]
    tiles_per_seq = seq_len // tm
    row = lambda w: pl.BlockSpec((tm, w), lambda i: (i, 0))
    vt_spec = pl.BlockSpec((1, N_ATTN_HEADS, ACC_ROWS, tm),
                           lambda i: (i // tiles_per_seq, 0, 0, i % tiles_per_seq))
    out_shape = (
        jax.ShapeDtypeStruct((n, d), F32),
        jax.ShapeDtypeStruct((n, ATTN_WIDTH), BF16),
        jax.ShapeDtypeStruct((n, 2 * ATTN_WIDTH), BF16),
        jax.ShapeDtypeStruct((n // seq_len, N_ATTN_HEADS, ACC_ROWS, seq_len), BF16),
        jax.ShapeDtypeStruct((n, SSM_WIDTH), F32),
        jax.ShapeDtypeStruct((n, CONV_DIM), F32),
        jax.ShapeDtypeStruct((n, LANES), F32),
    )
    return pl.pallas_call(
        functools.partial(_ffn_inproj_kernel, ff_chunk=ff_chunk, tiles_per_seq=tiles_per_seq),
        grid=(n // tm,),
        in_specs=[row(d), _resident((1, d)), _resident((d, d_ff)), _resident((d, d_ff)),
                  _resident((d_ff, d)), _resident((1, d)), _resident(win.shape), _resident(wdt.shape),
                  _resident((1, LANES)), _resident((1, LANES)), _resident((1, LANES))],
        out_specs=(row(d), row(ATTN_WIDTH), row(2 * ATTN_WIDTH), vt_spec, row(SSM_WIDTH),
                   row(CONV_DIM), row(LANES)),
        out_shape=out_shape,
        compiler_params=pltpu.CompilerParams(dimension_semantics=("arbitrary",),
                                             vmem_limit_bytes=VMEM_LIMIT_BYTES),
        name="ffn_inproj",
    )(x, n1, wg, wu, wd, nm, win, wdt, qg, kg, dtb)


KEY_BLOCKS_PER_TRIP = 4
MAX_STALE_EXCESS = 64.0


def _diff_attn_kernel(q_ref, kaug_ref, vt_ref, lq1_ref, lk1_ref, lq2_ref, lk2_ref, on_ref, o_ref,
                      acc_ref, p_ref, *, tq, tk, unroll, lam_init):
    qi = pl.program_id(2)
    n_terms = len(LOG2E_TERMS)

    q = q_ref[0]
    lane = lax.broadcasted_iota(jnp.int32, q.shape, 1)
    consts = jnp.zeros(q.shape, F32)
    for t, term in enumerate(LOG2E_TERMS):
        consts = jnp.where((lane == t) | (lane == n_terms + t), term, consts)
    consts = consts.astype(BF16)
    zero = jnp.zeros_like(q)
    qaug = (jnp.concatenate([jnp.where(lane < ATTN_HEAD_DIM, q, zero), consts], axis=1),
            jnp.concatenate([jnp.where(lane >= ATTN_HEAD_DIM, q, zero), consts], axis=1))
    qaug_t = [qa.astype(F32).T.astype(BF16) for qa in qaug]

    key = lax.broadcasted_iota(jnp.int32, (tk, tq), 0)
    qry = lax.broadcasted_iota(jnp.int32, (tk, tq), 1)

    def scores(j):
        kb = kaug_ref[0, pl.ds(pl.multiple_of(j * tk, tk), tk), :]
        return [_dot(kb, qaug_t[c]) for c in range(2)]

    def softmax_block(ss, ms, masked, stale):
        new, coefs, ps, excess = [], [], [], []
        for c in range(2):
            s = ss[c]
            if masked:
                s = jnp.where(key <= qry, s, NEG_BIG)
            block_max = jnp.max(s, axis=0, keepdims=True)
            m_new = jnp.maximum(ms[c], block_max)
            ps.append(jnp.exp2(s - (ms[c] if stale else m_new)).astype(BF16))
            rescale = jnp.exp2(ms[c] - m_new)
            coefs.append((rescale, rescale if stale else jnp.ones_like(rescale)))
            new.append(m_new)
            excess.append(block_max - ms[c])
        return tuple(new), tuple(coefs), ps, jnp.maximum(excess[0], excess[1])

    def accumulate(j, coefs, ps):
        vb = vt_ref[0, 0, :, pl.ds(pl.multiple_of(j * tk, tk), tk)]
        for c in range(2):
            a, b = coefs[c]
            acc_ref[c] = a * acc_ref[c] + b * _dot(vb, ps[c])

    def attend(stale, blocks_per_trip):
        def trip(n_blocks, stale, first=0):
            def body(t, carry):
                ms, coefs, worst = carry
                ps = [p_ref[0], p_ref[1]]
                for u in range(n_blocks):
                    j = first + t * n_blocks + u
                    ss = scores(j)
                    accumulate(jnp.where(j == 0, qi, j - 1), coefs, ps)
                    ms, coefs, ps, excess = softmax_block(ss, ms, False, stale)
                    if stale:
                        worst = jnp.maximum(worst, excess)
                p_ref[0], p_ref[1] = ps
                return ms, coefs, worst
            return body

        acc_ref[...] = jnp.zeros_like(acc_ref)
        m0 = jnp.full((1, tq), NEG_BIG, F32)
        ms, coefs, ps, _ = softmax_block(scores(qi), (m0, m0), True, False)
        p_ref[0], p_ref[1] = ps
        n_trips = qi // blocks_per_trip
        carry = lax.fori_loop(0, n_trips, trip(blocks_per_trip, stale), (ms, coefs, m0))
        done = n_trips * blocks_per_trip
        if blocks_per_trip > 2:
            n_pairs = (qi - done) // 2
            carry = lax.fori_loop(0, n_pairs, trip(2, stale, done), carry)
            done = done + 2 * n_pairs
        _, coefs, worst = lax.fori_loop(done, qi, trip(1, False), carry)
        accumulate(jnp.where(qi == 0, 0, qi - 1), coefs, [p_ref[0], p_ref[1]])
        return worst

    worst = attend(True, unroll)

    @pl.when(jnp.max(worst) > MAX_STALE_EXCESS)
    def _():
        attend(False, 1)

    lam = (jnp.exp(jnp.sum(lq1_ref[...] * lk1_ref[...], axis=-1, keepdims=True))
           - jnp.exp(jnp.sum(lq2_ref[...] * lk2_ref[...], axis=-1, keepdims=True)) + lam_init)
    a1, a2 = acc_ref[0], acc_ref[1]
    o_t = a1[0:LANES] / a1[LANES:LANES + 1] - lam * (a2[0:LANES] / a2[LANES:LANES + 1])
    o_ref[0] = (_rms(o_t.T, on_ref[...]) * (1.0 - lam_init)).astype(BF16)


def _diff_attn(q, kaug, vt, lq1, lk1, lq2, lk2, onorm, *, tq, tk, lam_init):
    b, s, _ = q.shape
    assert tq == tk, "the causal mask assumes square score blocks"
    qspec = pl.BlockSpec((1, tq, LANES), lambda bi, hi, qi: (bi, qi, hi))
    kspec = pl.BlockSpec((1, s, 2 * LANES), lambda bi, hi, qi: (bi, 0, hi))
    vspec = pl.BlockSpec((1, 1, ACC_ROWS, s), lambda bi, hi, qi: (bi, hi, 0, 0))
    small = lambda w: pl.BlockSpec((1, w), lambda bi, hi, qi: (0, 0))
    return pl.pallas_call(
        functools.partial(_diff_attn_kernel, tq=tq, tk=tk, unroll=KEY_BLOCKS_PER_TRIP,
                          lam_init=lam_init),
        grid=(b, N_ATTN_HEADS, s // tq),
        in_specs=[qspec, kspec, vspec, small(ATTN_HEAD_DIM), small(ATTN_HEAD_DIM),
                  small(ATTN_HEAD_DIM), small(ATTN_HEAD_DIM), small(LANES)],
        out_specs=qspec,
        out_shape=jax.ShapeDtypeStruct((b, s, ATTN_WIDTH), BF16),
        scratch_shapes=[pltpu.VMEM((2, ACC_ROWS, tq), F32),
                        pltpu.VMEM((2, tk, tq), BF16)],
        compiler_params=pltpu.CompilerParams(
            dimension_semantics=("arbitrary", "arbitrary", "arbitrary"),
            vmem_limit_bytes=VMEM_LIMIT_BYTES),
        name="diff_attn",
    )(q, kaug, vt, lq1, lk1, lq2, lk2, onorm)


def _split3(x):
    hi = x.astype(BF16)
    r = x - hi.astype(F32)
    mid = r.astype(BF16)
    lo = (r - mid.astype(F32)).astype(BF16)
    return hi, mid, lo


def _expand_heads(cols):
    rows = cols.shape[0]
    lane = lax.broadcasted_iota(jnp.int32, (rows, LANES), 1)
    parts = []
    for pr in range(N_SSM_HEADS // 2):
        a = jnp.broadcast_to(cols[:, 2 * pr:2 * pr + 1], (rows, LANES))
        b = jnp.broadcast_to(cols[:, 2 * pr + 1:2 * pr + 2], (rows, LANES))
        parts.append(jnp.where(lane < SSM_HEAD_DIM, a, b))
    return jnp.concatenate(parts, axis=1)


def _ssd_kernel(xbc_ref, z_ref, dt_ref, cw_ref, cb_ref, alog_ref, dskip_ref, on_ref, y_ref,
                xpad_ref, h_ref):
    c = pl.program_id(1)
    L = CHUNK
    halo = SUBLANES

    @pl.when(c == 0)
    def _():
        xpad_ref[0:halo, :] = jnp.zeros((halo, CONV_DIM), F32)
        h_ref[...] = jnp.zeros_like(h_ref)

    cur = xbc_ref[0]
    xpad_ref[halo:halo + L, :] = cur
    conv = cur * cw_ref[CONV_WIDTH - 1:CONV_WIDTH, :] + cb_ref[...]
    for j in range(CONV_WIDTH - 1):
        off = halo - (CONV_WIDTH - 1) + j
        conv = conv + xpad_ref[off:off + L, :] * cw_ref[j:j + 1, :]
    xpad_ref[0:halo, :] = cur[L - halo:, :]
    act = _silu(conv)
    xs = act[:, :SSM_WIDTH]

    dt = jax.nn.softplus(dt_ref[0])
    a = -jnp.exp(alog_ref[...])
    ad = dt * a

    r = lax.broadcasted_iota(jnp.int32, (L, L), 0)
    cc = lax.broadcasted_iota(jnp.int32, (L, L), 1)
    tril = r >= cc
    tri = jnp.where(tril, 1.0, 0.0).astype(BF16)
    hi, mid, lo = _split3(ad)
    a_cs = _dot(tri, hi) + _dot(tri, mid) + _dot(tri, lo)
    a_cs_t = a_cs.T

    dt_l = _expand_heads(dt)
    acs_l = _expand_heads(a_cs)
    last_l = acs_l[L - 1:L, :]
    xd = xs * dt_l
    xw = (xd * jnp.exp(last_l - acs_l)).astype(BF16)
    e_in = jnp.exp(acs_l)
    e_chunk_t = jnp.exp(a_cs_t[:, L - 1:L])
    xd16 = xd.astype(BF16)
    glane = lax.broadcasted_iota(jnp.int32, (L, GROUP_WIDTH), 1)

    ys = []
    for g in range(SSM_GROUPS):
        bg = act[:, SSM_WIDTH + g * SSM_STATE:SSM_WIDTH + (g + 1) * SSM_STATE].astype(BF16)
        cg = act[:, SSM_WIDTH + (SSM_GROUPS + g) * SSM_STATE:
                 SSM_WIDTH + (SSM_GROUPS + g + 1) * SSM_STATE].astype(BF16)
        gs = slice(g * GROUP_WIDTH, (g + 1) * GROUP_WIDTH)
        cb = _dot_nt(cg, bg)
        xd_g = xd16[:, gs]
        y_g = None
        for hl in range(HEADS_PER_GROUP):
            hd = g * HEADS_PER_GROUP + hl
            seg = a_cs[:, hd:hd + 1] - a_cs_t[hd:hd + 1, :]
            m = (cb * jnp.exp(jnp.where(tril, seg, NEG_BIG))).astype(BF16)
            in_head = (glane >= hl * SSM_HEAD_DIM) & (glane < (hl + 1) * SSM_HEAD_DIM)
            d = _dot(m, jnp.where(in_head, xd_g, jnp.zeros_like(xd_g)))
            y_g = d if y_g is None else y_g + d
        h_in = h_ref[g]
        y_off = _dot_nt(cg, h_in.astype(BF16)) * e_in[:, gs]
        ys.append(y_g + y_off)
        decay_rows = jnp.concatenate(
            [jnp.broadcast_to(e_chunk_t[g * HEADS_PER_GROUP + hl:g * HEADS_PER_GROUP + hl + 1, :],
                              (SSM_HEAD_DIM, SSM_STATE)) for hl in range(HEADS_PER_GROUP)], axis=0)
        h_ref[g] = h_in * decay_rows + _dot_tn(xw[:, gs], bg)

    y = jnp.concatenate(ys, axis=1) + dskip_ref[...] * xs
    y = y * _silu(z_ref[0])
    outs = []
    for g in range(SSM_GROUPS):
        gs = slice(g * GROUP_WIDTH, (g + 1) * GROUP_WIDTH)
        outs.append(_rms(y[:, gs], on_ref[:, gs]))
    y_ref[0] = jnp.concatenate(outs, axis=1).astype(BF16)


def _ssd(xbc, z, dt, cw, cb, alog, dskip, onorm):
    b, s, _ = xbc.shape
    blk = lambda w: pl.BlockSpec((1, CHUNK, w), lambda bi, ci: (bi, ci, 0))
    small = lambda r, w: pl.BlockSpec((r, w), lambda bi, ci: (0, 0))
    return pl.pallas_call(
        _ssd_kernel,
        grid=(b, s // CHUNK),
        in_specs=[blk(CONV_DIM), blk(SSM_WIDTH), blk(LANES), small(CONV_WIDTH, CONV_DIM),
                  small(1, CONV_DIM), small(1, LANES), small(1, SSM_WIDTH), small(1, SSM_WIDTH)],
        out_specs=blk(SSM_WIDTH),
        out_shape=jax.ShapeDtypeStruct((b, s, SSM_WIDTH), BF16),
        scratch_shapes=[pltpu.VMEM((SUBLANES + CHUNK, CONV_DIM), F32),
                        pltpu.VMEM((SSM_GROUPS, GROUP_WIDTH, SSM_STATE), F32)],
        compiler_params=pltpu.CompilerParams(dimension_semantics=("arbitrary", "arbitrary"),
                                             vmem_limit_bytes=VMEM_LIMIT_BYTES),
        name="ssd",
    )(xbc, z, dt, cw, cb, alog, dskip, onorm)


def _out_ffn_ple_kernel(x1_ref, oa_ref, ys_ref, p_ref, woa_ref, woy_ref, n2_ref, wg_ref, wu_ref,
                        wd_ref, ng_ref, wpg_ref, wpp_ref, npl_ref, out_ref, *, ff_chunk):
    x2 = x1_ref[...] + _dot(oa_ref[...], woa_ref[...]) + _dot(ys_ref[...], woy_ref[...])
    x3 = _swiglu_residual(x2, n2_ref[...], wg_ref, wu_ref, wd_ref, ff_chunk)
    e = _rms(_dot(p_ref[...].astype(BF16), wpp_ref[...]), npl_ref[...])
    gate = jax.nn.sigmoid(_dot(_rms(x3, ng_ref[...]).astype(BF16), wpg_ref[...]))
    out_ref[...] = x3 + gate * e


def _out_ffn_ple(x1, oa, ys, p, woa, woy, n2, wg, wu, wd, ng, wpg, wpp, npl, *, tm, ff_chunk):
    n, d = x1.shape
    d_ff = wg.shape[1]
    row = lambda w: pl.BlockSpec((tm, w), lambda i: (i, 0))
    return pl.pallas_call(
        functools.partial(_out_ffn_ple_kernel, ff_chunk=ff_chunk),
        grid=(n // tm,),
        in_specs=[row(d), row(ATTN_WIDTH), row(SSM_WIDTH), row(p.shape[1]),
                  _resident(woa.shape), _resident(woy.shape), _resident((1, d)),
                  _resident((d, d_ff)), _resident((d, d_ff)), _resident((d_ff, d)),
                  _resident((1, d)), _resident(wpg.shape), _resident(wpp.shape), _resident((1, d))],
        out_specs=row(d),
        out_shape=jax.ShapeDtypeStruct((n, d), F32),
        compiler_params=pltpu.CompilerParams(dimension_semantics=("arbitrary",),
                                             vmem_limit_bytes=VMEM_LIMIT_BYTES),
        name="out_ffn_ple",
    )(x1, oa, ys, p, woa, woy, n2, wg, wu, wd, ng, wpg, wpp, npl)


def _tiles(n, s, d_ff):
    tm = 512 if s % 512 == 0 else CHUNK
    tq = 512 if s % 512 == 0 else CHUNK
    tk = tq
    ff_chunk = 6 * MXU_WIDTH
    return tm, tq, tk, ff_chunk


def _layer(i, x, p_i, prm):
    b, s, d = x.shape
    n = b * s
    d_ff = prm["ffn1_w_gate"].shape[-1]
    tm, tq, tk, ff_chunk = _tiles(n, s, d_ff)
    lam_init = 0.8 - 0.6 * math.exp(-0.3 * i)
    row = lambda v: v.reshape(1, -1).astype(F32)
    w16 = lambda w: w.astype(BF16)

    w_in = prm["w_in"]
    n_main = 3 * ATTN_WIDTH + SSM_WIDTH + CONV_DIM
    win = w16(w_in[:, :n_main])
    wdt = w16(jnp.pad(w_in[:, n_main:], ((0, 0), (0, LANES - N_SSM_HEADS))))
    dtb = jnp.pad(row(prm["dt_bias"]), ((0, 0), (0, LANES - N_SSM_HEADS)))
    alog = jnp.pad(row(prm["a_log"]), ((0, 0), (0, LANES - N_SSM_HEADS)))
    qg = jnp.tile(row(prm["q_norm"]), (1, 2)) * (ATTN_HEAD_DIM ** -0.5 * LOG2E)
    kg = jnp.tile(row(prm["k_norm"]), (1, 2))
    dskip = jnp.repeat(row(prm["d_skip"]), SSM_HEAD_DIM, axis=1)
    w_out = prm["w_out"]

    x1, q, kaug, vt, z, xbc, dt = _ffn_inproj(
        x.reshape(n, d), row(prm["ffn1_norm"]), w16(prm["ffn1_w_gate"]), w16(prm["ffn1_w_up"]),
        w16(prm["ffn1_w_down"]), row(prm["mix_norm"]), win, wdt, qg, kg, dtb,
        seq_len=s, tm=tm, ff_chunk=ff_chunk)

    r3 = lambda t: t.reshape(b, s, t.shape[-1])
    oa = _diff_attn(r3(q), r3(kaug), vt, row(prm["lambda_q1"]), row(prm["lambda_k1"]),
                    row(prm["lambda_q2"]), row(prm["lambda_k2"]), row(prm["attn_out_norm"]),
                    tq=tq, tk=tk, lam_init=lam_init)
    ys = _ssd(r3(xbc), r3(z), r3(dt), prm["conv_w"].astype(F32), row(prm["conv_b"]), alog, dskip,
              row(prm["ssm_out_norm"]))

    out = _out_ffn_ple(
        x1, oa.reshape(n, ATTN_WIDTH), ys.reshape(n, SSM_WIDTH), p_i.reshape(n, p_i.shape[-1]),
        w16(w_out[:ATTN_WIDTH]), w16(w_out[ATTN_WIDTH:]), row(prm["ffn2_norm"]),
        w16(prm["ffn2_w_gate"]), w16(prm["ffn2_w_up"]), w16(prm["ffn2_w_down"]),
        row(prm["ple_gate_norm"]), w16(prm["w_ple_gate"]), w16(prm["w_ple_proj"]),
        row(prm["ple_norm"]), tm=tm, ff_chunk=ff_chunk)
    return out.reshape(b, s, d)


def kernel(x, p, ffn1_norm, ffn1_w_gate, ffn1_w_up, ffn1_w_down, mix_norm, w_in, q_norm, k_norm, lambda_q1, lambda_k1, lambda_q2, lambda_k2, attn_out_norm, conv_w, conv_b, dt_bias, a_log, d_skip, ssm_out_norm, w_out, ffn2_norm, ffn2_w_gate, ffn2_w_up, ffn2_w_down, ple_gate_norm, w_ple_gate, w_ple_proj, ple_norm):
    stacked = dict(
        ffn1_norm=ffn1_norm, ffn1_w_gate=ffn1_w_gate, ffn1_w_up=ffn1_w_up, ffn1_w_down=ffn1_w_down,
        mix_norm=mix_norm, w_in=w_in, q_norm=q_norm, k_norm=k_norm, lambda_q1=lambda_q1,
        lambda_k1=lambda_k1, lambda_q2=lambda_q2, lambda_k2=lambda_k2, attn_out_norm=attn_out_norm,
        conv_w=conv_w, conv_b=conv_b, dt_bias=dt_bias, a_log=a_log, d_skip=d_skip,
        ssm_out_norm=ssm_out_norm, w_out=w_out, ffn2_norm=ffn2_norm, ffn2_w_gate=ffn2_w_gate,
        ffn2_w_up=ffn2_w_up, ffn2_w_down=ffn2_w_down, ple_gate_norm=ple_gate_norm,
        w_ple_gate=w_ple_gate, w_ple_proj=w_ple_proj, ple_norm=ple_norm)
    for i in range(p.shape[0]):
        x = _layer(i, x, p[i], {name: w[i] for name, w in stacked.items()})
    return x
```

```python
import functools
import math

import numpy as np
import jax
import jax.numpy as jnp
from jax import lax
from jax.experimental import pallas as pl
from jax.experimental.pallas import tpu as pltpu

N_ATTN_HEADS = 4
ATTN_HEAD_DIM = 64
SSM_HEAD_DIM = 64
N_SSM_HEADS = 8
SSM_GROUPS = 2
SSM_STATE = 128
CONV_WIDTH = 4
CHUNK = 256
NORM_EPS = 1e-6

LANES = 128
SUBLANES = 8
MXU_WIDTH = 256
ROW_PARTS = 2
VMEM_LIMIT_BYTES = 56 * 1024 * 1024

ATTN_WIDTH = 2 * N_ATTN_HEADS * ATTN_HEAD_DIM
SSM_WIDTH = N_SSM_HEADS * SSM_HEAD_DIM
HEADS_PER_GROUP = N_SSM_HEADS // SSM_GROUPS
GROUP_WIDTH = HEADS_PER_GROUP * SSM_HEAD_DIM
CONV_DIM = SSM_WIDTH + 2 * SSM_GROUPS * SSM_STATE
NEG_BIG = -1e30

BF16 = jnp.bfloat16
F32 = jnp.float32


def _bf16_terms(x, n):
    terms = []
    for _ in range(n):
        t = float(np.asarray(x, dtype=BF16))
        terms.append(t)
        x -= t
    return terms


LOG2E = math.log2(math.e)
LOG2E_TERMS = _bf16_terms(LOG2E, 3)
POS_SPLIT = 128
ACC_ROWS = LANES + 16


def _rms(x, g):
    ms = jnp.mean(x * x, axis=-1, keepdims=True)
    return x * lax.rsqrt(ms + NORM_EPS) * g


def _silu(x):
    return x * jax.nn.sigmoid(x)


def _dot(a, b):
    return jnp.dot(a, b, preferred_element_type=F32)


def _dot_nt(a, b):
    return lax.dot_general(a, b, (((1,), (1,)), ((), ())), preferred_element_type=F32)


def _dot_tn(a, b):
    return lax.dot_general(a, b, (((0,), (0,)), ((), ())), preferred_element_type=F32)


def _resident(shape):
    zeros = (0,) * len(shape)
    return pl.BlockSpec(shape, lambda *_: zeros, pipeline_mode=pl.Buffered(1))


def _swiglu_residual(x, norm, wg_ref, wu_ref, wd_ref, ff_chunk):
    h = _rms(x, norm).astype(BF16)
    d_ff = wg_ref.shape[1]
    acc = None
    for lo in range(0, d_ff, ff_chunk):
        hi = min(lo + ff_chunk, d_ff)
        g = _dot(h, wg_ref[:, lo:hi])
        u = _dot(h, wu_ref[:, lo:hi])
        a = (_silu(g) * u).astype(BF16)
        d = _dot(a, wd_ref[lo:hi, :])
        acc = d if acc is None else acc + d
    return x + 0.5 * acc


def _pair_rms(x, g):
    lane = lax.broadcasted_iota(jnp.int32, x.shape, 1)
    lo = lane < ATTN_HEAD_DIM
    sq = x * x
    s_all = jnp.sum(sq, axis=-1, keepdims=True)
    s_lo = jnp.sum(jnp.where(lo, sq, 0.0), axis=-1, keepdims=True)
    s_hi = s_all - s_lo
    inv = 1.0 / ATTN_HEAD_DIM
    r = jnp.where(lo, lax.rsqrt(s_lo * inv + NORM_EPS), lax.rsqrt(s_hi * inv + NORM_EPS))
    return x * r * g


def _ffn_inproj_kernel(x_ref, n1_ref, wg_ref, wu_ref, wd_ref, nm_ref, win_ref, wdt_ref,
                       qg_ref, kg_ref, dtb_ref,
                       x1_ref, q_ref, kaug_ref, vt_ref, z_ref, xbc_ref, dt_ref, *,
                       ff_chunk, tiles_per_seq):
    aw = ATTN_WIDTH
    tm = x_ref.shape[0]
    part = tm // ROW_PARTS
    seq_start = (pl.program_id(0) % tiles_per_seq) * tm
    n_terms = len(LOG2E_TERMS)
    lane = lax.broadcasted_iota(jnp.int32, (part, LANES), 1)
    pad_row = lax.broadcasted_iota(jnp.int32, (ACC_ROWS - LANES, part), 0)
    ones_row = jnp.where(pad_row == 0, 1.0, 0.0).astype(BF16)
    for r in (slice(i * part, (i + 1) * part) for i in range(ROW_PARTS)):
        x1 = _swiglu_residual(x_ref[r, :], n1_ref[...], wg_ref, wu_ref, wd_ref, ff_chunk)
        x1_ref[r, :] = x1
        h = _rms(x1, nm_ref[...]).astype(BF16)
        q = _dot(h, win_ref[:, 0:aw])
        k = _dot(h, win_ref[:, aw:2 * aw])
        v = _dot(h, win_ref[:, 2 * aw:3 * aw])
        pos = seq_start + r.start + lax.broadcasted_iota(jnp.int32, (part, LANES), 0)
        pos_lo = pos & (POS_SPLIT - 1)
        pos_hi = (pos - pos_lo).astype(F32)
        pos_lo = pos_lo.astype(F32)
        for hd in range(N_ATTN_HEADS):
            sl = slice(hd * LANES, (hd + 1) * LANES)
            q_ref[r, sl] = _pair_rms(q[:, sl], qg_ref[...]).astype(BF16)
            slope = 2.0 ** (-8.0 * (hd + 1) / N_ATTN_HEADS)
            cols = jnp.where(lane < n_terms, slope * pos_hi,
                             jnp.where(lane < 2 * n_terms, slope * pos_lo, 0.0))
            kaug_ref[r, 2 * hd * LANES:(2 * hd + 1) * LANES] = (
                _pair_rms(k[:, sl], kg_ref[...]).astype(BF16))
            kaug_ref[r, (2 * hd + 1) * LANES:(2 * hd + 2) * LANES] = cols.astype(BF16)
            vt_ref[0, hd, 0:LANES, r] = v[:, sl].T.astype(BF16)
            vt_ref[0, hd, LANES:ACC_ROWS, r] = ones_row
        z_ref[r, :] = _dot(h, win_ref[:, 3 * aw:3 * aw + SSM_WIDTH])
        xbc_ref[r, :] = _dot(h, win_ref[:, 3 * aw + SSM_WIDTH:])
        dt_ref[r, :] = _dot(h, wdt_ref[...]) + dtb_ref[...]


def _ffn_inproj(x, n1, wg, wu, wd, nm, win, wdt, qg, kg, dtb, *, seq_len, tm, ff_chunk):
    n, d = x.shape
    d_ff = wg.shape[1]
    tiles_per_seq = seq_len // tm
    row = lambda w: pl.BlockSpec((tm, w), lambda i: (i, 0))
    vt_spec = pl.BlockSpec((1, N_ATTN_HEADS, ACC_ROWS, tm),
                           lambda i: (i // tiles_per_seq, 0, 0, i % tiles_per_seq))
    out_shape = (
        jax.ShapeDtypeStruct((n, d), F32),
        jax.ShapeDtypeStruct((n, ATTN_WIDTH), BF16),
        jax.ShapeDtypeStruct((n, 2 * ATTN_WIDTH), BF16),
        jax.ShapeDtypeStruct((n // seq_len, N_ATTN_HEADS, ACC_ROWS, seq_len), BF16),
        jax.ShapeDtypeStruct((n, SSM_WIDTH), F32),
        jax.ShapeDtypeStruct((n, CONV_DIM), F32),
        jax.ShapeDtypeStruct((n, LANES), F32),
    )
    return pl.pallas_call(
        functools.partial(_ffn_inproj_kernel, ff_chunk=ff_chunk, tiles_per_seq=tiles_per_seq),
        grid=(n // tm,),
        in_specs=[row(d), _resident((1, d)), _resident((d, d_ff)), _resident((d, d_ff)),
                  _resident((d_ff, d)), _resident((1, d)), _resident(win.shape), _resident(wdt.shape),
                  _resident((1, LANES)), _resident((1, LANES)), _resident((1, LANES))],
        out_specs=(row(d), row(ATTN_WIDTH), row(2 * ATTN_WIDTH), vt_spec, row(SSM_WIDTH),
                   row(CONV_DIM), row(LANES)),
        out_shape=out_shape,
        compiler_params=pltpu.CompilerParams(dimension_semantics=("arbitrary",),
                                             vmem_limit_bytes=VMEM_LIMIT_BYTES),
        name="ffn_inproj",
    )(x, n1, wg, wu, wd, nm, win, wdt, qg, kg, dtb)


KEY_BLOCKS_PER_TRIP = 4
QBLOCKS_PER_STEP = 2
MAX_STALE_EXCESS = 64.0


def _diff_attn_kernel(q_ref, kaug_ref, vt_ref, lq1_ref, lk1_ref, lq2_ref, lk2_ref, on_ref, o_ref,
                      acc2_ref, p2_ref, *, tq, tk, unroll, lam_init):
    lam = (jnp.exp(jnp.sum(lq1_ref[...] * lk1_ref[...], axis=-1, keepdims=True))
           - jnp.exp(jnp.sum(lq2_ref[...] * lk2_ref[...], axis=-1, keepdims=True)) + lam_init)

    def query_block(part):
        qi = QBLOCKS_PER_STEP * pl.program_id(2) + part
        rows = slice(part * tq, (part + 1) * tq)
        acc_ref, p_ref = acc2_ref.at[part], p2_ref.at[part]
        n_terms = len(LOG2E_TERMS)

        q = q_ref[0, rows, :]
        lane = lax.broadcasted_iota(jnp.int32, q.shape, 1)
        consts = jnp.zeros(q.shape, F32)
        for t, term in enumerate(LOG2E_TERMS):
            consts = jnp.where((lane == t) | (lane == n_terms + t), term, consts)
        consts = consts.astype(BF16)
        zero = jnp.zeros_like(q)
        qaug = (jnp.concatenate([jnp.where(lane < ATTN_HEAD_DIM, q, zero), consts], axis=1),
                jnp.concatenate([jnp.where(lane >= ATTN_HEAD_DIM, q, zero), consts], axis=1))
        qaug_t = [qa.astype(F32).T.astype(BF16) for qa in qaug]

        key = lax.broadcasted_iota(jnp.int32, (tk, tq), 0)
        qry = lax.broadcasted_iota(jnp.int32, (tk, tq), 1)

        def scores(j):
            kb = kaug_ref[0, pl.ds(pl.multiple_of(j * tk, tk), tk), :]
            return [_dot(kb, qaug_t[c]) for c in range(2)]

        def softmax_block(ss, ms, masked, stale):
            new, coefs, ps, excess = [], [], [], []
            for c in range(2):
                s = ss[c]
                if masked:
                    s = jnp.where(key <= qry, s, NEG_BIG)
                block_max = jnp.max(s, axis=0, keepdims=True)
                m_new = jnp.maximum(ms[c], block_max)
                ps.append(jnp.exp2(s - (ms[c] if stale else m_new)).astype(BF16))
                rescale = jnp.exp2(ms[c] - m_new)
                coefs.append((rescale, rescale if stale else jnp.ones_like(rescale)))
                new.append(m_new)
                excess.append(block_max - ms[c])
            return tuple(new), tuple(coefs), ps, jnp.maximum(excess[0], excess[1])

        def accumulate(j, coefs, ps):
            vb = vt_ref[0, 0, :, pl.ds(pl.multiple_of(j * tk, tk), tk)]
            for c in range(2):
                a, b = coefs[c]
                acc_ref[c] = a * acc_ref[c] + b * _dot(vb, ps[c])

        def attend(stale, blocks_per_trip):
            def trip(n_blocks, stale, first=0):
                def body(t, carry):
                    ms, coefs, worst = carry
                    ps = [p_ref[0], p_ref[1]]
                    for u in range(n_blocks):
                        j = first + t * n_blocks + u
                        ss = scores(j)
                        accumulate(jnp.where(j == 0, qi, j - 1), coefs, ps)
                        ms, coefs, ps, excess = softmax_block(ss, ms, False, stale)
                        if stale:
                            worst = jnp.maximum(worst, excess)
                    p_ref[0], p_ref[1] = ps
                    return ms, coefs, worst
                return body

            acc_ref[...] = jnp.zeros_like(acc_ref)
            m0 = jnp.full((1, tq), NEG_BIG, F32)
            ms, coefs, ps, _ = softmax_block(scores(qi), (m0, m0), True, False)
            p_ref[0], p_ref[1] = ps
            n_trips = qi // blocks_per_trip
            carry = lax.fori_loop(0, n_trips, trip(blocks_per_trip, stale), (ms, coefs, m0))
            done = n_trips * blocks_per_trip
            if blocks_per_trip > 2:
                n_pairs = (qi - done) // 2
                carry = lax.fori_loop(0, n_pairs, trip(2, stale, done), carry)
                done = done + 2 * n_pairs
            _, coefs, worst = lax.fori_loop(done, qi, trip(1, False), carry)
            accumulate(jnp.where(qi == 0, 0, qi - 1), coefs, [p_ref[0], p_ref[1]])
            return worst

        def finish():
            a1, a2 = acc_ref[0], acc_ref[1]
            o_t = a1[0:LANES] / a1[LANES:LANES + 1] - lam * (a2[0:LANES] / a2[LANES:LANES + 1])
            o_ref[0, rows, :] = (_rms(o_t.T, on_ref[...]) * (1.0 - lam_init)).astype(BF16)

        return attend, finish

    passes = []
    for part in range(QBLOCKS_PER_STEP):
        attend, finish = query_block(part)
        worst = attend(True, unroll)
        finish()
        passes.append((worst, attend, finish))

    for worst, attend, finish in passes:
        @pl.when(jnp.max(worst) > MAX_STALE_EXCESS)
        def _(attend=attend, finish=finish):
            attend(False, 1)
            finish()


def _diff_attn(q, kaug, vt, lq1, lk1, lq2, lk2, onorm, *, tq, tk, lam_init):
    b, s, _ = q.shape
    assert tq == tk, "the causal mask assumes square score blocks"
    step_rows = QBLOCKS_PER_STEP * tq
    assert s % step_rows == 0
    qspec = pl.BlockSpec((1, step_rows, LANES), lambda bi, hi, qi: (bi, qi, hi))
    kspec = pl.BlockSpec((1, s, 2 * LANES), lambda bi, hi, qi: (bi, 0, hi))
    vspec = pl.BlockSpec((1, 1, ACC_ROWS, s), lambda bi, hi, qi: (bi, hi, 0, 0))
    small = lambda w: pl.BlockSpec((1, w), lambda bi, hi, qi: (0, 0))
    return pl.pallas_call(
        functools.partial(_diff_attn_kernel, tq=tq, tk=tk, unroll=KEY_BLOCKS_PER_TRIP,
                          lam_init=lam_init),
        grid=(b, N_ATTN_HEADS, s // step_rows),
        in_specs=[qspec, kspec, vspec, small(ATTN_HEAD_DIM), small(ATTN_HEAD_DIM),
                  small(ATTN_HEAD_DIM), small(ATTN_HEAD_DIM), small(LANES)],
        out_specs=qspec,
        out_shape=jax.ShapeDtypeStruct((b, s, ATTN_WIDTH), BF16),
        scratch_shapes=[pltpu.VMEM((QBLOCKS_PER_STEP, 2, ACC_ROWS, tq), F32),
                        pltpu.VMEM((QBLOCKS_PER_STEP, 2, tk, tq), BF16)],
        compiler_params=pltpu.CompilerParams(
            dimension_semantics=("arbitrary", "arbitrary", "arbitrary"),
            vmem_limit_bytes=VMEM_LIMIT_BYTES),
        name="diff_attn",
    )(q, kaug, vt, lq1, lk1, lq2, lk2, onorm)


def _split3(x):
    hi = x.astype(BF16)
    r = x - hi.astype(F32)
    mid = r.astype(BF16)
    lo = (r - mid.astype(F32)).astype(BF16)
    return hi, mid, lo


def _expand_heads(cols):
    rows = cols.shape[0]
    lane = lax.broadcasted_iota(jnp.int32, (rows, LANES), 1)
    parts = []
    for pr in range(N_SSM_HEADS // 2):
        a = jnp.broadcast_to(cols[:, 2 * pr:2 * pr + 1], (rows, LANES))
        b = jnp.broadcast_to(cols[:, 2 * pr + 1:2 * pr + 2], (rows, LANES))
        parts.append(jnp.where(lane < SSM_HEAD_DIM, a, b))
    return jnp.concatenate(parts, axis=1)


def _ssd_kernel(xbc_ref, z_ref, dt_ref, cw_ref, cb_ref, alog_ref, dskip_ref, on_ref, y_ref,
                xpad_ref, h_ref):
    c = pl.program_id(1)
    L = CHUNK
    halo = SUBLANES

    @pl.when(c == 0)
    def _():
        xpad_ref[0:halo, :] = jnp.zeros((halo, CONV_DIM), F32)
        h_ref[...] = jnp.zeros_like(h_ref)

    cur = xbc_ref[0]
    xpad_ref[halo:halo + L, :] = cur
    conv = cur * cw_ref[CONV_WIDTH - 1:CONV_WIDTH, :] + cb_ref[...]
    for j in range(CONV_WIDTH - 1):
        off = halo - (CONV_WIDTH - 1) + j
        conv = conv + xpad_ref[off:off + L, :] * cw_ref[j:j + 1, :]
    xpad_ref[0:halo, :] = cur[L - halo:, :]
    act = _silu(conv)
    xs = act[:, :SSM_WIDTH]

    dt = jax.nn.softplus(dt_ref[0])
    a = -jnp.exp(alog_ref[...])
    ad = dt * a

    r = lax.broadcasted_iota(jnp.int32, (L, L), 0)
    cc = lax.broadcasted_iota(jnp.int32, (L, L), 1)
    tril = r >= cc
    tri = jnp.where(tril, 1.0, 0.0).astype(BF16)
    hi, mid, lo = _split3(ad)
    a_cs = _dot(tri, hi) + _dot(tri, mid) + _dot(tri, lo)
    a_cs_t = a_cs.T

    dt_l = _expand_heads(dt)
    acs_l = _expand_heads(a_cs)
    last_l = acs_l[L - 1:L, :]
    xd = xs * dt_l
    xw = (xd * jnp.exp(last_l - acs_l)).astype(BF16)
    e_in = jnp.exp(acs_l)
    e_chunk_t = jnp.exp(a_cs_t[:, L - 1:L])
    xd16 = xd.astype(BF16)
    glane = lax.broadcasted_iota(jnp.int32, (L, GROUP_WIDTH), 1)

    ys = []
    for g in range(SSM_GROUPS):
        bg = act[:, SSM_WIDTH + g * SSM_STATE:SSM_WIDTH + (g + 1) * SSM_STATE].astype(BF16)
        cg = act[:, SSM_WIDTH + (SSM_GROUPS + g) * SSM_STATE:
                 SSM_WIDTH + (SSM_GROUPS + g + 1) * SSM_STATE].astype(BF16)
        gs = slice(g * GROUP_WIDTH, (g + 1) * GROUP_WIDTH)
        cb = _dot_nt(cg, bg)
        xd_g = xd16[:, gs]
        y_g = None
        for hl in range(HEADS_PER_GROUP):
            hd = g * HEADS_PER_GROUP + hl
            seg = a_cs[:, hd:hd + 1] - a_cs_t[hd:hd + 1, :]
            m = (cb * jnp.exp(jnp.where(tril, seg, NEG_BIG))).astype(BF16)
            in_head = (glane >= hl * SSM_HEAD_DIM) & (glane < (hl + 1) * SSM_HEAD_DIM)
            d = _dot(m, jnp.where(in_head, xd_g, jnp.zeros_like(xd_g)))
            y_g = d if y_g is None else y_g + d
        h_in = h_ref[g]
        y_off = _dot_nt(cg, h_in.astype(BF16)) * e_in[:, gs]
        ys.append(y_g + y_off)
        decay_rows = jnp.concatenate(
            [jnp.broadcast_to(e_chunk_t[g * HEADS_PER_GROUP + hl:g * HEADS_PER_GROUP + hl + 1, :],
                              (SSM_HEAD_DIM, SSM_STATE)) for hl in range(HEADS_PER_GROUP)], axis=0)
        h_ref[g] = h_in * decay_rows + _dot_tn(xw[:, gs], bg)

    y = jnp.concatenate(ys, axis=1) + dskip_ref[...] * xs
    y = y * _silu(z_ref[0])
    outs = []
    for g in range(SSM_GROUPS):
        gs = slice(g * GROUP_WIDTH, (g + 1) * GROUP_WIDTH)
        outs.append(_rms(y[:, gs], on_ref[:, gs]))
    y_ref[0] = jnp.concatenate(outs, axis=1).astype(BF16)


def _ssd(xbc, z, dt, cw, cb, alog, dskip, onorm):
    b, s, _ = xbc.shape
    blk = lambda w: pl.BlockSpec((1, CHUNK, w), lambda bi, ci: (bi, ci, 0))
    small = lambda r, w: pl.BlockSpec((r, w), lambda bi, ci: (0, 0))
    return pl.pallas_call(
        _ssd_kernel,
        grid=(b, s // CHUNK),
        in_specs=[blk(CONV_DIM), blk(SSM_WIDTH), blk(LANES), small(CONV_WIDTH, CONV_DIM),
                  small(1, CONV_DIM), small(1, LANES), small(1, SSM_WIDTH), small(1, SSM_WIDTH)],
        out_specs=blk(SSM_WIDTH),
        out_shape=jax.ShapeDtypeStruct((b, s, SSM_WIDTH), BF16),
        scratch_shapes=[pltpu.VMEM((SUBLANES + CHUNK, CONV_DIM), F32),
                        pltpu.VMEM((SSM_GROUPS, GROUP_WIDTH, SSM_STATE), F32)],
        compiler_params=pltpu.CompilerParams(dimension_semantics=("arbitrary", "arbitrary"),
                                             vmem_limit_bytes=VMEM_LIMIT_BYTES),
        name="ssd",
    )(xbc, z, dt, cw, cb, alog, dskip, onorm)


def _out_ffn_ple_kernel(x1_ref, oa_ref, ys_ref, p_ref, woa_ref, woy_ref, n2_ref, wg_ref, wu_ref,
                        wd_ref, ng_ref, wpg_ref, wpp_ref, npl_ref, out_ref, *, ff_chunk):
    x2 = x1_ref[...] + _dot(oa_ref[...], woa_ref[...]) + _dot(ys_ref[...], woy_ref[...])
    x3 = _swiglu_residual(x2, n2_ref[...], wg_ref, wu_ref, wd_ref, ff_chunk)
    e = _rms(_dot(p_ref[...].astype(BF16), wpp_ref[...]), npl_ref[...])
    gate = jax.nn.sigmoid(_dot(_rms(x3, ng_ref[...]).astype(BF16), wpg_ref[...]))
    out_ref[...] = x3 + gate * e


def _out_ffn_ple(x1, oa, ys, p, woa, woy, n2, wg, wu, wd, ng, wpg, wpp, npl, *, tm, ff_chunk):
    n, d = x1.shape
    d_ff = wg.shape[1]
    row = lambda w: pl.BlockSpec((tm, w), lambda i: (i, 0))
    return pl.pallas_call(
        functools.partial(_out_ffn_ple_kernel, ff_chunk=ff_chunk),
        grid=(n // tm,),
        in_specs=[row(d), row(ATTN_WIDTH), row(SSM_WIDTH), row(p.shape[1]),
                  _resident(woa.shape), _resident(woy.shape), _resident((1, d)),
                  _resident((d, d_ff)), _resident((d, d_ff)), _resident((d_ff, d)),
                  _resident((1, d)), _resident(wpg.shape), _resident(wpp.shape), _resident((1, d))],
        out_specs=row(d),
        out_shape=jax.ShapeDtypeStruct((n, d), F32),
        compiler_params=pltpu.CompilerParams(dimension_semantics=("arbitrary",),
                                             vmem_limit_bytes=VMEM_LIMIT_BYTES),
        name="out_ffn_ple",
    )(x1, oa, ys, p, woa, woy, n2, wg, wu, wd, ng, wpg, wpp, npl)


def _tiles(n, s, d_ff):
    tm = 512 if s % 512 == 0 else CHUNK
    tq = 512 if s % 512 == 0 else CHUNK
    tk = tq
    ff_chunk = 6 * MXU_WIDTH
    return tm, tq, tk, ff_chunk


def _layer(i, x, p_i, prm):
    b, s, d = x.shape
    n = b * s
    d_ff = prm["ffn1_w_gate"].shape[-1]
    tm, tq, tk, ff_chunk = _tiles(n, s, d_ff)
    lam_init = 0.8 - 0.6 * math.exp(-0.3 * i)
    row = lambda v: v.reshape(1, -1).astype(F32)
    w16 = lambda w: w.astype(BF16)

    w_in = prm["w_in"]
    n_main = 3 * ATTN_WIDTH + SSM_WIDTH + CONV_DIM
    win = w16(w_in[:, :n_main])
    wdt = w16(jnp.pad(w_in[:, n_main:], ((0, 0), (0, LANES - N_SSM_HEADS))))
    dtb = jnp.pad(row(prm["dt_bias"]), ((0, 0), (0, LANES - N_SSM_HEADS)))
    alog = jnp.pad(row(prm["a_log"]), ((0, 0), (0, LANES - N_SSM_HEADS)))
    qg = jnp.tile(row(prm["q_norm"]), (1, 2)) * (ATTN_HEAD_DIM ** -0.5 * LOG2E)
    kg = jnp.tile(row(prm["k_norm"]), (1, 2))
    dskip = jnp.repeat(row(prm["d_skip"]), SSM_HEAD_DIM, axis=1)
    w_out = prm["w_out"]

    x1, q, kaug, vt, z, xbc, dt = _ffn_inproj(
        x.reshape(n, d), row(prm["ffn1_norm"]), w16(prm["ffn1_w_gate"]), w16(prm["ffn1_w_up"]),
        w16(prm["ffn1_w_down"]), row(prm["mix_norm"]), win, wdt, qg, kg, dtb,
        seq_len=s, tm=tm, ff_chunk=ff_chunk)

    r3 = lambda t: t.reshape(b, s, t.shape[-1])
    oa = _diff_attn(r3(q), r3(kaug), vt, row(prm["lambda_q1"]), row(prm["lambda_k1"]),
                    row(prm["lambda_q2"]), row(prm["lambda_k2"]), row(prm["attn_out_norm"]),
                    tq=tq, tk=tk, lam_init=lam_init)
    ys = _ssd(r3(xbc), r3(z), r3(dt), prm["conv_w"].astype(F32), row(prm["conv_b"]), alog, dskip,
              row(prm["ssm_out_norm"]))

    out = _out_ffn_ple(
        x1, oa.reshape(n, ATTN_WIDTH), ys.reshape(n, SSM_WIDTH), p_i.reshape(n, p_i.shape[-1]),
        w16(w_out[:ATTN_WIDTH]), w16(w_out[ATTN_WIDTH:]), row(prm["ffn2_norm"]),
        w16(prm["ffn2_w_gate"]), w16(prm["ffn2_w_up"]), w16(prm["ffn2_w_down"]),
        row(prm["ple_gate_norm"]), w16(prm["w_ple_gate"]), w16(prm["w_ple_proj"]),
        row(prm["ple_norm"]), tm=tm, ff_chunk=ff_chunk)
    return out.reshape(b, s, d)


def kernel(x, p, ffn1_norm, ffn1_w_gate, ffn1_w_up, ffn1_w_down, mix_norm, w_in, q_norm, k_norm, lambda_q1, lambda_k1, lambda_q2, lambda_k2, attn_out_norm, conv_w, conv_b, dt_bias, a_log, d_skip, ssm_out_norm, w_out, ffn2_norm, ffn2_w_gate, ffn2_w_up, ffn2_w_down, ple_gate_norm, w_ple_gate, w_ple_proj, ple_norm):
    stacked = dict(
        ffn1_norm=ffn1_norm, ffn1_w_gate=ffn1_w_gate, ffn1_w_up=ffn1_w_up, ffn1_w_down=ffn1_w_down,
        mix_norm=mix_norm, w_in=w_in, q_norm=q_norm, k_norm=k_norm, lambda_q1=lambda_q1,
        lambda_k1=lambda_k1, lambda_q2=lambda_q2, lambda_k2=lambda_k2, attn_out_norm=attn_out_norm,
        conv_w=conv_w, conv_b=conv_b, dt_bias=dt_bias, a_log=a_log, d_skip=d_skip,
        ssm_out_norm=ssm_out_norm, w_out=w_out, ffn2_norm=ffn2_norm, ffn2_w_gate=ffn2_w_gate,
        ffn2_w_up=ffn2_w_up, ffn2_w_down=ffn2_w_down, ple_gate_norm=ple_gate_norm,
        w_ple_gate=w_ple_gate, w_ple_proj=w_ple_proj, ple_norm=ple_norm)
    for i in range(p.shape[0]):
        x = _layer(i, x, p[i], {name: w[i] for name, w in stacked.items()})
    return x
```

```python
import functools
import math

import numpy as np
import jax
import jax.numpy as jnp
from jax import lax
from jax.experimental import pallas as pl
from jax.experimental.pallas import tpu as pltpu

N_ATTN_HEADS = 4
ATTN_HEAD_DIM = 64
SSM_HEAD_DIM = 64
N_SSM_HEADS = 8
SSM_GROUPS = 2
SSM_STATE = 128
CONV_WIDTH = 4
CHUNK = 256
NORM_EPS = 1e-6

LANES = 128
SUBLANES = 8
MXU_WIDTH = 256
ROW_PARTS = 2
VMEM_LIMIT_BYTES = 56 * 1024 * 1024

ATTN_WIDTH = 2 * N_ATTN_HEADS * ATTN_HEAD_DIM
SSM_WIDTH = N_SSM_HEADS * SSM_HEAD_DIM
HEADS_PER_GROUP = N_SSM_HEADS // SSM_GROUPS
GROUP_WIDTH = HEADS_PER_GROUP * SSM_HEAD_DIM
CONV_DIM = SSM_WIDTH + 2 * SSM_GROUPS * SSM_STATE
NEG_BIG = -1e30

BF16 = jnp.bfloat16
F32 = jnp.float32


def _bf16_terms(x, n):
    terms = []
    for _ in range(n):
        t = float(np.asarray(x, dtype=BF16))
        terms.append(t)
        x -= t
    return terms


LOG2E = math.log2(math.e)
LOG2E_TERMS = _bf16_terms(LOG2E, 3)
POS_SPLIT = 128
ACC_ROWS = LANES + 16


def _rms(x, g):
    ms = jnp.mean(x * x, axis=-1, keepdims=True)
    return x * lax.rsqrt(ms + NORM_EPS) * g


def _silu(x):
    return x * jax.nn.sigmoid(x)


def _dot(a, b):
    return jnp.dot(a, b, preferred_element_type=F32)


def _dot_nt(a, b):
    return lax.dot_general(a, b, (((1,), (1,)), ((), ())), preferred_element_type=F32)


def _dot_tn(a, b):
    return lax.dot_general(a, b, (((0,), (0,)), ((), ())), preferred_element_type=F32)


def _resident(shape):
    zeros = (0,) * len(shape)
    return pl.BlockSpec(shape, lambda *_: zeros, pipeline_mode=pl.Buffered(1))


def _swiglu_residual(x, norm, wg_ref, wu_ref, wd_ref, ff_chunk):
    h = _rms(x, norm).astype(BF16)
    d_ff = wg_ref.shape[1]
    acc = None
    for lo in range(0, d_ff, ff_chunk):
        hi = min(lo + ff_chunk, d_ff)
        g = _dot(h, wg_ref[:, lo:hi])
        u = _dot(h, wu_ref[:, lo:hi])
        a = (_silu(g) * u).astype(BF16)
        d = _dot(a, wd_ref[lo:hi, :])
        acc = d if acc is None else acc + d
    return x + 0.5 * acc


def _pair_rms(x, g):
    lane = lax.broadcasted_iota(jnp.int32, x.shape, 1)
    lo = lane < ATTN_HEAD_DIM
    sq = x * x
    s_all = jnp.sum(sq, axis=-1, keepdims=True)
    s_lo = jnp.sum(jnp.where(lo, sq, 0.0), axis=-1, keepdims=True)
    s_hi = s_all - s_lo
    inv = 1.0 / ATTN_HEAD_DIM
    r = jnp.where(lo, lax.rsqrt(s_lo * inv + NORM_EPS), lax.rsqrt(s_hi * inv + NORM_EPS))
    return x * r * g


def _ffn_inproj_kernel(x_ref, n1_ref, wg_ref, wu_ref, wd_ref, nm_ref, win_ref, wdt_ref,
                       qg_ref, kg_ref, dtb_ref,
                       x1_ref, q_ref, kaug_ref, vt_ref, z_ref, xbc_ref, dt_ref, *,
                       ff_chunk, tiles_per_seq):
    aw = ATTN_WIDTH
    tm = x_ref.shape[0]
    part = tm // ROW_PARTS
    seq_start = (pl.program_id(0) % tiles_per_seq) * tm
    n_terms = len(LOG2E_TERMS)
    lane = lax.broadcasted_iota(jnp.int32, (part, LANES), 1)
    pad_row = lax.broadcasted_iota(jnp.int32, (ACC_ROWS - LANES, part), 0)
    ones_row = jnp.where(pad_row == 0, 1.0, 0.0).astype(BF16)
    for r in (slice(i * part, (i + 1) * part) for i in range(ROW_PARTS)):
        x1 = _swiglu_residual(x_ref[r, :], n1_ref[...], wg_ref, wu_ref, wd_ref, ff_chunk)
        x1_ref[r, :] = x1
        h = _rms(x1, nm_ref[...]).astype(BF16)
        q = _dot(h, win_ref[:, 0:aw])
        k = _dot(h, win_ref[:, aw:2 * aw])
        v = _dot(h, win_ref[:, 2 * aw:3 * aw])
        pos = seq_start + r.start + lax.broadcasted_iota(jnp.int32, (part, LANES), 0)
        pos_lo = pos & (POS_SPLIT - 1)
        pos_hi = (pos - pos_lo).astype(F32)
        pos_lo = pos_lo.astype(F32)
        for hd in range(N_ATTN_HEADS):
            sl = slice(hd * LANES, (hd + 1) * LANES)
            q_ref[r, sl] = _pair_rms(q[:, sl], qg_ref[...]).astype(BF16)
            slope = 2.0 ** (-8.0 * (hd + 1) / N_ATTN_HEADS)
            cols = jnp.where(lane < n_terms, slope * pos_hi,
                             jnp.where(lane < 2 * n_terms, slope * pos_lo, 0.0))
            kaug_ref[r, 2 * hd * LANES:(2 * hd + 1) * LANES] = (
                _pair_rms(k[:, sl], kg_ref[...]).astype(BF16))
            kaug_ref[r, (2 * hd + 1) * LANES:(2 * hd + 2) * LANES] = cols.astype(BF16)
            vt_ref[0, hd, 0:LANES, r] = v[:, sl].T.astype(BF16)
            vt_ref[0, hd, LANES:ACC_ROWS, r] = ones_row
        z_ref[r, :] = _dot(h, win_ref[:, 3 * aw:3 * aw + SSM_WIDTH])
        xbc_ref[r, :] = _dot(h, win_ref[:, 3 * aw + SSM_WIDTH:])
        dt_ref[r, :] = _dot(h, wdt_ref[...]) + dtb_ref[...]


def _ffn_inproj(x, n1, wg, wu, wd, nm, win, wdt, qg, kg, dtb, *, seq_len, tm, ff_chunk):
    n, d = x.shape
    d_ff = wg.shape[1]
    tiles_per_seq = seq_len // tm
    row = lambda w: pl.BlockSpec((tm, w), lambda i: (i, 0))
    vt_spec = pl.BlockSpec((1, N_ATTN_HEADS, ACC_ROWS, tm),
                           lambda i: (i // tiles_per_seq, 0, 0, i % tiles_per_seq))
    out_shape = (
        jax.ShapeDtypeStruct((n, d), F32),
        jax.ShapeDtypeStruct((n, ATTN_WIDTH), BF16),
        jax.ShapeDtypeStruct((n, 2 * ATTN_WIDTH), BF16),
        jax.ShapeDtypeStruct((n // seq_len, N_ATTN_HEADS, ACC_ROWS, seq_len), BF16),
        jax.ShapeDtypeStruct((n, SSM_WIDTH), F32),
        jax.ShapeDtypeStruct((n, CONV_DIM), F32),
        jax.ShapeDtypeStruct((n, LANES), F32),
    )
    return pl.pallas_call(
        functools.partial(_ffn_inproj_kernel, ff_chunk=ff_chunk, tiles_per_seq=tiles_per_seq),
        grid=(n // tm,),
        in_specs=[row(d), _resident((1, d)), _resident((d, d_ff)), _resident((d, d_ff)),
                  _resident((d_ff, d)), _resident((1, d)), _resident(win.shape), _resident(wdt.shape),
                  _resident((1, LANES)), _resident((1, LANES)), _resident((1, LANES))],
        out_specs=(row(d), row(ATTN_WIDTH), row(2 * ATTN_WIDTH), vt_spec, row(SSM_WIDTH),
                   row(CONV_DIM), row(LANES)),
        out_shape=out_shape,
        compiler_params=pltpu.CompilerParams(dimension_semantics=("arbitrary",),
                                             vmem_limit_bytes=VMEM_LIMIT_BYTES),
        name="ffn_inproj",
    )(x, n1, wg, wu, wd, nm, win, wdt, qg, kg, dtb)


KEY_BLOCKS_PER_TRIP = 4
QBLOCKS_PER_STEP = 4
MAX_STALE_EXCESS = 64.0


def _diff_attn_kernel(q_ref, kaug_ref, vt_ref, lq1_ref, lk1_ref, lq2_ref, lk2_ref, on_ref, o_ref,
                      acc2_ref, p2_ref, *, tq, tk, unroll, lam_init):
    lam = (jnp.exp(jnp.sum(lq1_ref[...] * lk1_ref[...], axis=-1, keepdims=True))
           - jnp.exp(jnp.sum(lq2_ref[...] * lk2_ref[...], axis=-1, keepdims=True)) + lam_init)

    def query_block(part):
        qi = QBLOCKS_PER_STEP * pl.program_id(2) + part
        rows = slice(part * tq, (part + 1) * tq)
        acc_ref, p_ref = acc2_ref.at[part], p2_ref.at[part]
        n_terms = len(LOG2E_TERMS)

        q = q_ref[0, rows, :]
        lane = lax.broadcasted_iota(jnp.int32, q.shape, 1)
        consts = jnp.zeros(q.shape, F32)
        for t, term in enumerate(LOG2E_TERMS):
            consts = jnp.where((lane == t) | (lane == n_terms + t), term, consts)
        consts = consts.astype(BF16)
        zero = jnp.zeros_like(q)
        qaug = (jnp.concatenate([jnp.where(lane < ATTN_HEAD_DIM, q, zero), consts], axis=1),
                jnp.concatenate([jnp.where(lane >= ATTN_HEAD_DIM, q, zero), consts], axis=1))
        qaug_t = [qa.astype(F32).T.astype(BF16) for qa in qaug]

        key = lax.broadcasted_iota(jnp.int32, (tk, tq), 0)
        qry = lax.broadcasted_iota(jnp.int32, (tk, tq), 1)

        def scores(j):
            kb = kaug_ref[0, pl.ds(pl.multiple_of(j * tk, tk), tk), :]
            return [_dot(kb, qaug_t[c]) for c in range(2)]

        def softmax_block(ss, ms, masked, stale):
            new, coefs, ps, excess = [], [], [], []
            for c in range(2):
                s = ss[c]
                if masked:
                    s = jnp.where(key <= qry, s, NEG_BIG)
                block_max = jnp.max(s, axis=0, keepdims=True)
                m_new = jnp.maximum(ms[c], block_max)
                ps.append(jnp.exp2(s - (ms[c] if stale else m_new)).astype(BF16))
                rescale = jnp.exp2(ms[c] - m_new)
                coefs.append((rescale, rescale if stale else jnp.ones_like(rescale)))
                new.append(m_new)
                excess.append(block_max - ms[c])
            return tuple(new), tuple(coefs), ps, jnp.maximum(excess[0], excess[1])

        def accumulate(j, coefs, ps):
            vb = vt_ref[0, 0, :, pl.ds(pl.multiple_of(j * tk, tk), tk)]
            for c in range(2):
                a, b = coefs[c]
                acc_ref[c] = a * acc_ref[c] + b * _dot(vb, ps[c])

        def attend(stale, blocks_per_trip):
            def trip(n_blocks, stale, first=0):
                def body(t, carry):
                    ms, coefs, worst = carry
                    ps = [p_ref[0], p_ref[1]]
                    for u in range(n_blocks):
                        j = first + t * n_blocks + u
                        ss = scores(j)
                        accumulate(jnp.where(j == 0, qi, j - 1), coefs, ps)
                        ms, coefs, ps, excess = softmax_block(ss, ms, False, stale)
                        if stale:
                            worst = jnp.maximum(worst, excess)
                    p_ref[0], p_ref[1] = ps
                    return ms, coefs, worst
                return body

            acc_ref[...] = jnp.zeros_like(acc_ref)
            m0 = jnp.full((1, tq), NEG_BIG, F32)
            ms, coefs, ps, _ = softmax_block(scores(qi), (m0, m0), True, False)
            p_ref[0], p_ref[1] = ps
            n_trips = qi // blocks_per_trip
            carry = lax.fori_loop(0, n_trips, trip(blocks_per_trip, stale), (ms, coefs, m0))
            done = n_trips * blocks_per_trip
            if blocks_per_trip > 2:
                n_pairs = (qi - done) // 2
                carry = lax.fori_loop(0, n_pairs, trip(2, stale, done), carry)
                done = done + 2 * n_pairs
            _, coefs, worst = lax.fori_loop(done, qi, trip(1, False), carry)
            accumulate(jnp.where(qi == 0, 0, qi - 1), coefs, [p_ref[0], p_ref[1]])
            return worst

        def finish():
            a1, a2 = acc_ref[0], acc_ref[1]
            o_t = a1[0:LANES] / a1[LANES:LANES + 1] - lam * (a2[0:LANES] / a2[LANES:LANES + 1])
            o_ref[0, rows, :] = (_rms(o_t.T, on_ref[...]) * (1.0 - lam_init)).astype(BF16)

        return attend, finish

    passes = []
    for part in range(QBLOCKS_PER_STEP):
        attend, finish = query_block(part)
        worst = attend(True, unroll)
        finish()
        passes.append((worst, attend, finish))

    for worst, attend, finish in passes:
        @pl.when(jnp.max(worst) > MAX_STALE_EXCESS)
        def _(attend=attend, finish=finish):
            attend(False, 1)
            finish()


def _diff_attn(q, kaug, vt, lq1, lk1, lq2, lk2, onorm, *, tq, tk, lam_init):
    b, s, _ = q.shape
    assert tq == tk, "the causal mask assumes square score blocks"
    step_rows = QBLOCKS_PER_STEP * tq
    assert s % step_rows == 0
    qspec = pl.BlockSpec((1, step_rows, LANES), lambda bi, hi, qi: (bi, qi, hi))
    kspec = pl.BlockSpec((1, s, 2 * LANES), lambda bi, hi, qi: (bi, 0, hi))
    vspec = pl.BlockSpec((1, 1, ACC_ROWS, s), lambda bi, hi, qi: (bi, hi, 0, 0))
    small = lambda w: pl.BlockSpec((1, w), lambda bi, hi, qi: (0, 0))
    return pl.pallas_call(
        functools.partial(_diff_attn_kernel, tq=tq, tk=tk, unroll=KEY_BLOCKS_PER_TRIP,
                          lam_init=lam_init),
        grid=(b, N_ATTN_HEADS, s // step_rows),
        in_specs=[qspec, kspec, vspec, small(ATTN_HEAD_DIM), small(ATTN_HEAD_DIM),
                  small(ATTN_HEAD_DIM), small(ATTN_HEAD_DIM), small(LANES)],
        out_specs=qspec,
        out_shape=jax.ShapeDtypeStruct((b, s, ATTN_WIDTH), BF16),
        scratch_shapes=[pltpu.VMEM((QBLOCKS_PER_STEP, 2, ACC_ROWS, tq), F32),
                        pltpu.VMEM((QBLOCKS_PER_STEP, 2, tk, tq), BF16)],
        compiler_params=pltpu.CompilerParams(
            dimension_semantics=("arbitrary", "arbitrary", "arbitrary"),
            vmem_limit_bytes=VMEM_LIMIT_BYTES),
        name="diff_attn",
    )(q, kaug, vt, lq1, lk1, lq2, lk2, onorm)


def _split3(x):
    hi = x.astype(BF16)
    r = x - hi.astype(F32)
    mid = r.astype(BF16)
    lo = (r - mid.astype(F32)).astype(BF16)
    return hi, mid, lo


def _expand_heads(cols):
    rows = cols.shape[0]
    lane = lax.broadcasted_iota(jnp.int32, (rows, LANES), 1)
    parts = []
    for pr in range(N_SSM_HEADS // 2):
        a = jnp.broadcast_to(cols[:, 2 * pr:2 * pr + 1], (rows, LANES))
        b = jnp.broadcast_to(cols[:, 2 * pr + 1:2 * pr + 2], (rows, LANES))
        parts.append(jnp.where(lane < SSM_HEAD_DIM, a, b))
    return jnp.concatenate(parts, axis=1)


def _ssd_kernel(xbc_ref, z_ref, dt_ref, cw_ref, cb_ref, alog_ref, dskip_ref, on_ref, y_ref,
                xpad_ref, h_ref):
    c = pl.program_id(1)
    L = CHUNK
    halo = SUBLANES

    @pl.when(c == 0)
    def _():
        xpad_ref[0:halo, :] = jnp.zeros((halo, CONV_DIM), F32)
        h_ref[...] = jnp.zeros_like(h_ref)

    cur = xbc_ref[0]
    xpad_ref[halo:halo + L, :] = cur
    conv = cur * cw_ref[CONV_WIDTH - 1:CONV_WIDTH, :] + cb_ref[...]
    for j in range(CONV_WIDTH - 1):
        off = halo - (CONV_WIDTH - 1) + j
        conv = conv + xpad_ref[off:off + L, :] * cw_ref[j:j + 1, :]
    xpad_ref[0:halo, :] = cur[L - halo:, :]
    act = _silu(conv)
    xs = act[:, :SSM_WIDTH]

    dt = jax.nn.softplus(dt_ref[0])
    a = -jnp.exp(alog_ref[...])
    ad = dt * a

    r = lax.broadcasted_iota(jnp.int32, (L, L), 0)
    cc = lax.broadcasted_iota(jnp.int32, (L, L), 1)
    tril = r >= cc
    tri = jnp.where(tril, 1.0, 0.0).astype(BF16)
    hi, mid, lo = _split3(ad)
    a_cs = _dot(tri, hi) + _dot(tri, mid) + _dot(tri, lo)
    a_cs_t = a_cs.T

    dt_l = _expand_heads(dt)
    acs_l = _expand_heads(a_cs)
    last_l = acs_l[L - 1:L, :]
    xd = xs * dt_l
    xw = (xd * jnp.exp(last_l - acs_l)).astype(BF16)
    e_in = jnp.exp(acs_l)
    e_chunk_t = jnp.exp(a_cs_t[:, L - 1:L])
    xd16 = xd.astype(BF16)
    glane = lax.broadcasted_iota(jnp.int32, (L, GROUP_WIDTH), 1)

    ys = []
    for g in range(SSM_GROUPS):
        bg = act[:, SSM_WIDTH + g * SSM_STATE:SSM_WIDTH + (g + 1) * SSM_STATE].astype(BF16)
        cg = act[:, SSM_WIDTH + (SSM_GROUPS + g) * SSM_STATE:
                 SSM_WIDTH + (SSM_GROUPS + g + 1) * SSM_STATE].astype(BF16)
        gs = slice(g * GROUP_WIDTH, (g + 1) * GROUP_WIDTH)
        cb = _dot_nt(cg, bg)
        xd_g = xd16[:, gs]
        y_g = None
        for hl in range(HEADS_PER_GROUP):
            hd = g * HEADS_PER_GROUP + hl
            seg = a_cs[:, hd:hd + 1] - a_cs_t[hd:hd + 1, :]
            m = (cb * jnp.exp(jnp.where(tril, seg, NEG_BIG))).astype(BF16)
            in_head = (glane >= hl * SSM_HEAD_DIM) & (glane < (hl + 1) * SSM_HEAD_DIM)
            d = _dot(m, jnp.where(in_head, xd_g, jnp.zeros_like(xd_g)))
            y_g = d if y_g is None else y_g + d
        h_in = h_ref[g]
        y_off = _dot_nt(cg, h_in.astype(BF16)) * e_in[:, gs]
        ys.append(y_g + y_off)
        decay_rows = jnp.concatenate(
            [jnp.broadcast_to(e_chunk_t[g * HEADS_PER_GROUP + hl:g * HEADS_PER_GROUP + hl + 1, :],
                              (SSM_HEAD_DIM, SSM_STATE)) for hl in range(HEADS_PER_GROUP)], axis=0)
        h_ref[g] = h_in * decay_rows + _dot_tn(xw[:, gs], bg)

    y = jnp.concatenate(ys, axis=1) + dskip_ref[...] * xs
    y = y * _silu(z_ref[0])
    outs = []
    for g in range(SSM_GROUPS):
        gs = slice(g * GROUP_WIDTH, (g + 1) * GROUP_WIDTH)
        outs.append(_rms(y[:, gs], on_ref[:, gs]))
    y_ref[0] = jnp.concatenate(outs, axis=1).astype(BF16)


def _ssd(xbc, z, dt, cw, cb, alog, dskip, onorm):
    b, s, _ = xbc.shape
    blk = lambda w: pl.BlockSpec((1, CHUNK, w), lambda bi, ci: (bi, ci, 0))
    small = lambda r, w: pl.BlockSpec((r, w), lambda bi, ci: (0, 0))
    return pl.pallas_call(
        _ssd_kernel,
        grid=(b, s // CHUNK),
        in_specs=[blk(CONV_DIM), blk(SSM_WIDTH), blk(LANES), small(CONV_WIDTH, CONV_DIM),
                  small(1, CONV_DIM), small(1, LANES), small(1, SSM_WIDTH), small(1, SSM_WIDTH)],
        out_specs=blk(SSM_WIDTH),
        out_shape=jax.ShapeDtypeStruct((b, s, SSM_WIDTH), BF16),
        scratch_shapes=[pltpu.VMEM((SUBLANES + CHUNK, CONV_DIM), F32),
                        pltpu.VMEM((SSM_GROUPS, GROUP_WIDTH, SSM_STATE), F32)],
        compiler_params=pltpu.CompilerParams(dimension_semantics=("arbitrary", "arbitrary"),
                                             vmem_limit_bytes=VMEM_LIMIT_BYTES),
        name="ssd",
    )(xbc, z, dt, cw, cb, alog, dskip, onorm)


def _out_ffn_ple_kernel(x1_ref, oa_ref, ys_ref, p_ref, woa_ref, woy_ref, n2_ref, wg_ref, wu_ref,
                        wd_ref, ng_ref, wpg_ref, wpp_ref, npl_ref, out_ref, *, ff_chunk):
    x2 = x1_ref[...] + _dot(oa_ref[...], woa_ref[...]) + _dot(ys_ref[...], woy_ref[...])
    x3 = _swiglu_residual(x2, n2_ref[...], wg_ref, wu_ref, wd_ref, ff_chunk)
    e = _rms(_dot(p_ref[...].astype(BF16), wpp_ref[...]), npl_ref[...])
    gate = jax.nn.sigmoid(_dot(_rms(x3, ng_ref[...]).astype(BF16), wpg_ref[...]))
    out_ref[...] = x3 + gate * e


def _out_ffn_ple(x1, oa, ys, p, woa, woy, n2, wg, wu, wd, ng, wpg, wpp, npl, *, tm, ff_chunk):
    n, d = x1.shape
    d_ff = wg.shape[1]
    row = lambda w: pl.BlockSpec((tm, w), lambda i: (i, 0))
    return pl.pallas_call(
        functools.partial(_out_ffn_ple_kernel, ff_chunk=ff_chunk),
        grid=(n // tm,),
        in_specs=[row(d), row(ATTN_WIDTH), row(SSM_WIDTH), row(p.shape[1]),
                  _resident(woa.shape), _resident(woy.shape), _resident((1, d)),
                  _resident((d, d_ff)), _resident((d, d_ff)), _resident((d_ff, d)),
                  _resident((1, d)), _resident(wpg.shape), _resident(wpp.shape), _resident((1, d))],
        out_specs=row(d),
        out_shape=jax.ShapeDtypeStruct((n, d), F32),
        compiler_params=pltpu.CompilerParams(dimension_semantics=("arbitrary",),
                                             vmem_limit_bytes=VMEM_LIMIT_BYTES),
        name="out_ffn_ple",
    )(x1, oa, ys, p, woa, woy, n2, wg, wu, wd, ng, wpg, wpp, npl)


def _tiles(n, s, d_ff):
    tm = 512 if s % 512 == 0 else CHUNK
    tq = 512 if s % 512 == 0 else CHUNK
    tk = tq
    ff_chunk = 6 * MXU_WIDTH
    return tm, tq, tk, ff_chunk


def _layer(i, x, p_i, prm):
    b, s, d = x.shape
    n = b * s
    d_ff = prm["ffn1_w_gate"].shape[-1]
    tm, tq, tk, ff_chunk = _tiles(n, s, d_ff)
    lam_init = 0.8 - 0.6 * math.exp(-0.3 * i)
    row = lambda v: v.reshape(1, -1).astype(F32)
    w16 = lambda w: w.astype(BF16)

    w_in = prm["w_in"]
    n_main = 3 * ATTN_WIDTH + SSM_WIDTH + CONV_DIM
    win = w16(w_in[:, :n_main])
    wdt = w16(jnp.pad(w_in[:, n_main:], ((0, 0), (0, LANES - N_SSM_HEADS))))
    dtb = jnp.pad(row(prm["dt_bias"]), ((0, 0), (0, LANES - N_SSM_HEADS)))
    alog = jnp.pad(row(prm["a_log"]), ((0, 0), (0, LANES - N_SSM_HEADS)))
    qg = jnp.tile(row(prm["q_norm"]), (1, 2)) * (ATTN_HEAD_DIM ** -0.5 * LOG2E)
    kg = jnp.tile(row(prm["k_norm"]), (1, 2))
    dskip = jnp.repeat(row(prm["d_skip"]), SSM_HEAD_DIM, axis=1)
    w_out = prm["w_out"]

    x1, q, kaug, vt, z, xbc, dt = _ffn_inproj(
        x.reshape(n, d), row(prm["ffn1_norm"]), w16(prm["ffn1_w_gate"]), w16(prm["ffn1_w_up"]),
        w16(prm["ffn1_w_down"]), row(prm["mix_norm"]), win, wdt, qg, kg, dtb,
        seq_len=s, tm=tm, ff_chunk=ff_chunk)

    r3 = lambda t: t.reshape(b, s, t.shape[-1])
    oa = _diff_attn(r3(q), r3(kaug), vt, row(prm["lambda_q1"]), row(prm["lambda_k1"]),
                    row(prm["lambda_q2"]), row(prm["lambda_k2"]), row(prm["attn_out_norm"]),
                    tq=tq, tk=tk, lam_init=lam_init)
    ys = _ssd(r3(xbc), r3(z), r3(dt), prm["conv_w"].astype(F32), row(prm["conv_b"]), alog, dskip,
              row(prm["ssm_out_norm"]))

    out = _out_ffn_ple(
        x1, oa.reshape(n, ATTN_WIDTH), ys.reshape(n, SSM_WIDTH), p_i.reshape(n, p_i.shape[-1]),
        w16(w_out[:ATTN_WIDTH]), w16(w_out[ATTN_WIDTH:]), row(prm["ffn2_norm"]),
        w16(prm["ffn2_w_gate"]), w16(prm["ffn2_w_up"]), w16(prm["ffn2_w_down"]),
        row(prm["ple_gate_norm"]), w16(prm["w_ple_gate"]), w16(prm["w_ple_proj"]),
        row(prm["ple_norm"]), tm=tm, ff_chunk=ff_chunk)
    return out.reshape(b, s, d)


def kernel(x, p, ffn1_norm, ffn1_w_gate, ffn1_w_up, ffn1_w_down, mix_norm, w_in, q_norm, k_norm, lambda_q1, lambda_k1, lambda_q2, lambda_k2, attn_out_norm, conv_w, conv_b, dt_bias, a_log, d_skip, ssm_out_norm, w_out, ffn2_norm, ffn2_w_gate, ffn2_w_up, ffn2_w_down, ple_gate_norm, w_ple_gate, w_ple_proj, ple_norm):
    stacked = dict(
        ffn1_norm=ffn1_norm, ffn1_w_gate=ffn1_w_gate, ffn1_w_up=ffn1_w_up, ffn1_w_down=ffn1_w_down,
        mix_norm=mix_norm, w_in=w_in, q_norm=q_norm, k_norm=k_norm, lambda_q1=lambda_q1,
        lambda_k1=lambda_k1, lambda_q2=lambda_q2, lambda_k2=lambda_k2, attn_out_norm=attn_out_norm,
        conv_w=conv_w, conv_b=conv_b, dt_bias=dt_bias, a_log=a_log, d_skip=d_skip,
        ssm_out_norm=ssm_out_norm, w_out=w_out, ffn2_norm=ffn2_norm, ffn2_w_gate=ffn2_w_gate,
        ffn2_w_up=ffn2_w_up, ffn2_w_down=ffn2_w_down, ple_gate_norm=ple_gate_norm,
        w_ple_gate=w_ple_gate, w_ple_proj=w_ple_proj, ple_norm=ple_norm)
    for i in range(p.shape[0]):
        x = _layer(i, x, p[i], {name: w[i] for name, w in stacked.items()})
    return x
```

```python
import functools
import math

import numpy as np
import jax
import jax.numpy as jnp
from jax import lax
from jax.experimental import pallas as pl
from jax.experimental.pallas import tpu as pltpu

N_ATTN_HEADS = 4
ATTN_HEAD_DIM = 64
SSM_HEAD_DIM = 64
N_SSM_HEADS = 8
SSM_GROUPS = 2
SSM_STATE = 128
CONV_WIDTH = 4
CHUNK = 256
NORM_EPS = 1e-6

LANES = 128
SUBLANES = 8
MXU_WIDTH = 256
ROW_PARTS = 2
VMEM_LIMIT_BYTES = 56 * 1024 * 1024

ATTN_WIDTH = 2 * N_ATTN_HEADS * ATTN_HEAD_DIM
SSM_WIDTH = N_SSM_HEADS * SSM_HEAD_DIM
HEADS_PER_GROUP = N_SSM_HEADS // SSM_GROUPS
GROUP_WIDTH = HEADS_PER_GROUP * SSM_HEAD_DIM
CONV_DIM = SSM_WIDTH + 2 * SSM_GROUPS * SSM_STATE
NEG_BIG = -1e30

BF16 = jnp.bfloat16
F32 = jnp.float32


def _bf16_terms(x, n):
    terms = []
    for _ in range(n):
        t = float(np.asarray(x, dtype=BF16))
        terms.append(t)
        x -= t
    return terms


LOG2E = math.log2(math.e)
LOG2E_TERMS = _bf16_terms(LOG2E, 3)
POS_SPLIT = 128
ACC_ROWS = LANES + 16


def _rms(x, g):
    ms = jnp.mean(x * x, axis=-1, keepdims=True)
    return x * lax.rsqrt(ms + NORM_EPS) * g


def _silu(x):
    return x * jax.nn.sigmoid(x)


def _dot(a, b):
    return jnp.dot(a, b, preferred_element_type=F32)


def _dot_nt(a, b):
    return lax.dot_general(a, b, (((1,), (1,)), ((), ())), preferred_element_type=F32)


def _dot_tn(a, b):
    return lax.dot_general(a, b, (((0,), (0,)), ((), ())), preferred_element_type=F32)


def _resident(shape):
    zeros = (0,) * len(shape)
    return pl.BlockSpec(shape, lambda *_: zeros, pipeline_mode=pl.Buffered(1))


def _swiglu_residual(x, norm, wg_ref, wu_ref, wd_ref, ff_chunk):
    h = _rms(x, norm).astype(BF16)
    d_ff = wg_ref.shape[1]
    acc = None
    for lo in range(0, d_ff, ff_chunk):
        hi = min(lo + ff_chunk, d_ff)
        g = _dot(h, wg_ref[:, lo:hi])
        u = _dot(h, wu_ref[:, lo:hi])
        a = (_silu(g) * u).astype(BF16)
        d = _dot(a, wd_ref[lo:hi, :])
        acc = d if acc is None else acc + d
    return x + 0.5 * acc


def _pair_rms(x, g):
    lane = lax.broadcasted_iota(jnp.int32, x.shape, 1)
    lo = lane < ATTN_HEAD_DIM
    sq = x * x
    s_all = jnp.sum(sq, axis=-1, keepdims=True)
    s_lo = jnp.sum(jnp.where(lo, sq, 0.0), axis=-1, keepdims=True)
    s_hi = s_all - s_lo
    inv = 1.0 / ATTN_HEAD_DIM
    r = jnp.where(lo, lax.rsqrt(s_lo * inv + NORM_EPS), lax.rsqrt(s_hi * inv + NORM_EPS))
    return x * r * g


def _ffn_inproj_kernel(x_ref, n1_ref, wg_ref, wu_ref, wd_ref, nm_ref, win_ref, wdt_ref,
                       qg_ref, kg_ref, dtb_ref,
                       x1_ref, q_ref, kaug_ref, vt_ref, z_ref, xbc_ref, dt_ref, *,
                       ff_chunk, tiles_per_seq):
    aw = ATTN_WIDTH
    tm = x_ref.shape[0]
    part = tm // ROW_PARTS
    seq_start = (pl.program_id(0) % tiles_per_seq) * tm
    n_terms = len(LOG2E_TERMS)
    lane = lax.broadcasted_iota(jnp.int32, (part, LANES), 1)
    pad_row = lax.broadcasted_iota(jnp.int32, (ACC_ROWS - LANES, part), 0)
    ones_row = jnp.where(pad_row == 0, 1.0, 0.0).astype(BF16)
    for r in (slice(i * part, (i + 1) * part) for i in range(ROW_PARTS)):
        x1 = _swiglu_residual(x_ref[r, :], n1_ref[...], wg_ref, wu_ref, wd_ref, ff_chunk)
        x1_ref[r, :] = x1
        h = _rms(x1, nm_ref[...]).astype(BF16)
        q = _dot(h, win_ref[:, 0:aw])
        k = _dot(h, win_ref[:, aw:2 * aw])
        v = _dot(h, win_ref[:, 2 * aw:3 * aw])
        pos = seq_start + r.start + lax.broadcasted_iota(jnp.int32, (part, LANES), 0)
        pos_lo = pos & (POS_SPLIT - 1)
        pos_cols = jnp.where(lane < n_terms, pos - pos_lo,
                             jnp.where(lane < 2 * n_terms, pos_lo, 0)).astype(F32)
        for hd in range(N_ATTN_HEADS):
            sl = slice(hd * LANES, (hd + 1) * LANES)
            q_ref[r, sl] = _pair_rms(q[:, sl], qg_ref[...]).astype(BF16)
            slope = 2.0 ** (-8.0 * (hd + 1) / N_ATTN_HEADS)
            kaug_ref[r, 2 * hd * LANES:(2 * hd + 1) * LANES] = (
                _pair_rms(k[:, sl], kg_ref[...]).astype(BF16))
            kaug_ref[r, (2 * hd + 1) * LANES:(2 * hd + 2) * LANES] = (slope * pos_cols).astype(BF16)
            vt_ref[0, hd, 0:LANES, r] = v[:, sl].T.astype(BF16)
            vt_ref[0, hd, LANES:ACC_ROWS, r] = ones_row
        z_ref[r, :] = _dot(h, win_ref[:, 3 * aw:3 * aw + SSM_WIDTH])
        xbc_ref[r, :] = _dot(h, win_ref[:, 3 * aw + SSM_WIDTH:])
        dt_ref[r, :] = _dot(h, wdt_ref[...]) + dtb_ref[...]


def _ffn_inproj(x, n1, wg, wu, wd, nm, win, wdt, qg, kg, dtb, *, seq_len, tm, ff_chunk):
    n, d = x.shape
    d_ff = wg.shape[1]
    tiles_per_seq = seq_len // tm
    row = lambda w: pl.BlockSpec((tm, w), lambda i: (i, 0))
    vt_spec = pl.BlockSpec((1, N_ATTN_HEADS, ACC_ROWS, tm),
                           lambda i: (i // tiles_per_seq, 0, 0, i % tiles_per_seq))
    out_shape = (
        jax.ShapeDtypeStruct((n, d), F32),
        jax.ShapeDtypeStruct((n, ATTN_WIDTH), BF16),
        jax.ShapeDtypeStruct((n, 2 * ATTN_WIDTH), BF16),
        jax.ShapeDtypeStruct((n // seq_len, N_ATTN_HEADS, ACC_ROWS, seq_len), BF16),
        jax.ShapeDtypeStruct((n, SSM_WIDTH), F32),
        jax.ShapeDtypeStruct((n, CONV_DIM), F32),
        jax.ShapeDtypeStruct((n, LANES), F32),
    )
    return pl.pallas_call(
        functools.partial(_ffn_inproj_kernel, ff_chunk=ff_chunk, tiles_per_seq=tiles_per_seq),
        grid=(n // tm,),
        in_specs=[row(d), _resident((1, d)), _resident((d, d_ff)), _resident((d, d_ff)),
                  _resident((d_ff, d)), _resident((1, d)), _resident(win.shape), _resident(wdt.shape),
                  _resident((1, LANES)), _resident((1, LANES)), _resident((1, LANES))],
        out_specs=(row(d), row(ATTN_WIDTH), row(2 * ATTN_WIDTH), vt_spec, row(SSM_WIDTH),
                   row(CONV_DIM), row(LANES)),
        out_shape=out_shape,
        compiler_params=pltpu.CompilerParams(dimension_semantics=("arbitrary",),
                                             vmem_limit_bytes=VMEM_LIMIT_BYTES),
        name="ffn_inproj",
    )(x, n1, wg, wu, wd, nm, win, wdt, qg, kg, dtb)


KEY_BLOCKS_PER_TRIP = 4
QBLOCKS_PER_STEP = 4
MAX_STALE_EXCESS = 64.0


def _diff_attn_kernel(q_ref, kaug_ref, vt_ref, lq1_ref, lk1_ref, lq2_ref, lk2_ref, on_ref, o_ref,
                      acc2_ref, p2_ref, *, tq, tk, unroll, lam_init):
    lam = (jnp.exp(jnp.sum(lq1_ref[...] * lk1_ref[...], axis=-1, keepdims=True))
           - jnp.exp(jnp.sum(lq2_ref[...] * lk2_ref[...], axis=-1, keepdims=True)) + lam_init)

    def query_block(part):
        qi = QBLOCKS_PER_STEP * pl.program_id(2) + part
        rows = slice(part * tq, (part + 1) * tq)
        acc_ref, p_ref = acc2_ref.at[part], p2_ref.at[part]
        n_terms = len(LOG2E_TERMS)

        q = q_ref[0, rows, :]
        lane = lax.broadcasted_iota(jnp.int32, q.shape, 1)
        consts = jnp.zeros(q.shape, F32)
        for t, term in enumerate(LOG2E_TERMS):
            consts = jnp.where((lane == t) | (lane == n_terms + t), term, consts)
        consts = consts.astype(BF16)
        zero = jnp.zeros_like(q)
        qaug = (jnp.concatenate([jnp.where(lane < ATTN_HEAD_DIM, q, zero), consts], axis=1),
                jnp.concatenate([jnp.where(lane >= ATTN_HEAD_DIM, q, zero), consts], axis=1))
        qaug_t = [qa.astype(F32).T.astype(BF16) for qa in qaug]

        key = lax.broadcasted_iota(jnp.int32, (tk, tq), 0)
        qry = lax.broadcasted_iota(jnp.int32, (tk, tq), 1)

        def scores(j):
            kb = kaug_ref[0, pl.ds(pl.multiple_of(j * tk, tk), tk), :]
            return [_dot(kb, qaug_t[c]) for c in range(2)]

        def softmax_block(ss, ms, masked, stale):
            new, coefs, ps, excess = [], [], [], []
            for c in range(2):
                s = ss[c]
                if masked:
                    s = jnp.where(key <= qry, s, NEG_BIG)
                block_max = jnp.max(s, axis=0, keepdims=True)
                m_new = jnp.maximum(ms[c], block_max)
                ps.append(jnp.exp2(s - (ms[c] if stale else m_new)).astype(BF16))
                rescale = jnp.exp2(ms[c] - m_new)
                coefs.append((rescale, rescale if stale else jnp.ones_like(rescale)))
                new.append(m_new)
                excess.append(block_max - ms[c])
            return tuple(new), tuple(coefs), ps, jnp.maximum(excess[0], excess[1])

        def accumulate(j, coefs, ps):
            vb = vt_ref[0, 0, :, pl.ds(pl.multiple_of(j * tk, tk), tk)]
            for c in range(2):
                a, b = coefs[c]
                acc_ref[c] = a * acc_ref[c] + b * _dot(vb, ps[c])

        def attend(stale, blocks_per_trip):
            def trip(n_blocks, stale, first=0):
                def body(t, carry):
                    ms, coefs, worst = carry
                    ps = [p_ref[0], p_ref[1]]
                    for u in range(n_blocks):
                        j = first + t * n_blocks + u
                        ss = scores(j)
                        accumulate(jnp.where(j == 0, qi, j - 1), coefs, ps)
                        ms, coefs, ps, excess = softmax_block(ss, ms, False, stale)
                        if stale:
                            worst = jnp.maximum(worst, excess)
                    p_ref[0], p_ref[1] = ps
                    return ms, coefs, worst
                return body

            acc_ref[...] = jnp.zeros_like(acc_ref)
            m0 = jnp.full((1, tq), NEG_BIG, F32)
            ms, coefs, ps, _ = softmax_block(scores(qi), (m0, m0), True, False)
            p_ref[0], p_ref[1] = ps
            n_trips = qi // blocks_per_trip
            carry = lax.fori_loop(0, n_trips, trip(blocks_per_trip, stale), (ms, coefs, m0))
            done = n_trips * blocks_per_trip
            if blocks_per_trip > 2:
                n_pairs = (qi - done) // 2
                carry = lax.fori_loop(0, n_pairs, trip(2, stale, done), carry)
                done = done + 2 * n_pairs
            _, coefs, worst = lax.fori_loop(done, qi, trip(1, False), carry)
            accumulate(jnp.where(qi == 0, 0, qi - 1), coefs, [p_ref[0], p_ref[1]])
            return worst

        def finish():
            a1, a2 = acc_ref[0], acc_ref[1]
            o_t = a1[0:LANES] / a1[LANES:LANES + 1] - lam * (a2[0:LANES] / a2[LANES:LANES + 1])
            o_ref[0, rows, :] = (_rms(o_t.T, on_ref[...]) * (1.0 - lam_init)).astype(BF16)

        return attend, finish

    passes = []
    for part in range(QBLOCKS_PER_STEP):
        attend, finish = query_block(part)
        worst = attend(True, unroll)
        finish()
        passes.append((worst, attend, finish))

    for worst, attend, finish in passes:
        @pl.when(jnp.max(worst) > MAX_STALE_EXCESS)
        def _(attend=attend, finish=finish):
            attend(False, 1)
            finish()


def _diff_attn(q, kaug, vt, lq1, lk1, lq2, lk2, onorm, *, tq, tk, lam_init):
    b, s, _ = q.shape
    assert tq == tk, "the causal mask assumes square score blocks"
    step_rows = QBLOCKS_PER_STEP * tq
    assert s % step_rows == 0
    qspec = pl.BlockSpec((1, step_rows, LANES), lambda bi, hi, qi: (bi, qi, hi))
    kspec = pl.BlockSpec((1, s, 2 * LANES), lambda bi, hi, qi: (bi, 0, hi))
    vspec = pl.BlockSpec((1, 1, ACC_ROWS, s), lambda bi, hi, qi: (bi, hi, 0, 0))
    small = lambda w: pl.BlockSpec((1, w), lambda bi, hi, qi: (0, 0))
    return pl.pallas_call(
        functools.partial(_diff_attn_kernel, tq=tq, tk=tk, unroll=KEY_BLOCKS_PER_TRIP,
                          lam_init=lam_init),
        grid=(b, N_ATTN_HEADS, s // step_rows),
        in_specs=[qspec, kspec, vspec, small(ATTN_HEAD_DIM), small(ATTN_HEAD_DIM),
                  small(ATTN_HEAD_DIM), small(ATTN_HEAD_DIM), small(LANES)],
        out_specs=qspec,
        out_shape=jax.ShapeDtypeStruct((b, s, ATTN_WIDTH), BF16),
        scratch_shapes=[pltpu.VMEM((QBLOCKS_PER_STEP, 2, ACC_ROWS, tq), F32),
                        pltpu.VMEM((QBLOCKS_PER_STEP, 2, tk, tq), BF16)],
        compiler_params=pltpu.CompilerParams(
            dimension_semantics=("arbitrary", "arbitrary", "arbitrary"),
            vmem_limit_bytes=VMEM_LIMIT_BYTES),
        name="diff_attn",
    )(q, kaug, vt, lq1, lk1, lq2, lk2, onorm)


def _split3(x):
    hi = x.astype(BF16)
    r = x - hi.astype(F32)
    mid = r.astype(BF16)
    lo = (r - mid.astype(F32)).astype(BF16)
    return hi, mid, lo


def _expand_heads(cols):
    rows = cols.shape[0]
    lane = lax.broadcasted_iota(jnp.int32, (rows, LANES), 1)
    parts = []
    for pr in range(N_SSM_HEADS // 2):
        a = jnp.broadcast_to(cols[:, 2 * pr:2 * pr + 1], (rows, LANES))
        b = jnp.broadcast_to(cols[:, 2 * pr + 1:2 * pr + 2], (rows, LANES))
        parts.append(jnp.where(lane < SSM_HEAD_DIM, a, b))
    return jnp.concatenate(parts, axis=1)


def _ssd_kernel(xbc_ref, z_ref, dt_ref, cw_ref, cb_ref, alog_ref, dskip_ref, on_ref, y_ref,
                xpad_ref, h_ref):
    c = pl.program_id(1)
    L = CHUNK
    halo = SUBLANES

    @pl.when(c == 0)
    def _():
        xpad_ref[0:halo, :] = jnp.zeros((halo, CONV_DIM), F32)
        h_ref[...] = jnp.zeros_like(h_ref)

    cur = xbc_ref[0]
    xpad_ref[halo:halo + L, :] = cur
    conv = cur * cw_ref[CONV_WIDTH - 1:CONV_WIDTH, :] + cb_ref[...]
    for j in range(CONV_WIDTH - 1):
        off = halo - (CONV_WIDTH - 1) + j
        conv = conv + xpad_ref[off:off + L, :] * cw_ref[j:j + 1, :]
    xpad_ref[0:halo, :] = cur[L - halo:, :]
    act = _silu(conv)
    xs = act[:, :SSM_WIDTH]

    dt = jax.nn.softplus(dt_ref[0])
    a = -jnp.exp(alog_ref[...])
    ad = dt * a

    r = lax.broadcasted_iota(jnp.int32, (L, L), 0)
    cc = lax.broadcasted_iota(jnp.int32, (L, L), 1)
    tril = r >= cc
    tri = jnp.where(tril, 1.0, 0.0).astype(BF16)
    hi, mid, lo = _split3(ad)
    a_cs = _dot(tri, hi) + _dot(tri, mid) + _dot(tri, lo)
    a_cs_t = a_cs.T

    dt_l = _expand_heads(dt)
    acs_l = _expand_heads(a_cs)
    last_l = acs_l[L - 1:L, :]
    xd = xs * dt_l
    xw = (xd * jnp.exp(last_l - acs_l)).astype(BF16)
    e_in = jnp.exp(acs_l)
    e_chunk_t = jnp.exp(a_cs_t[:, L - 1:L])
    xd16 = xd.astype(BF16)
    glane = lax.broadcasted_iota(jnp.int32, (L, GROUP_WIDTH), 1)

    ys = []
    for g in range(SSM_GROUPS):
        bg = act[:, SSM_WIDTH + g * SSM_STATE:SSM_WIDTH + (g + 1) * SSM_STATE].astype(BF16)
        cg = act[:, SSM_WIDTH + (SSM_GROUPS + g) * SSM_STATE:
                 SSM_WIDTH + (SSM_GROUPS + g + 1) * SSM_STATE].astype(BF16)
        gs = slice(g * GROUP_WIDTH, (g + 1) * GROUP_WIDTH)
        cb = _dot_nt(cg, bg)
        xd_g = xd16[:, gs]
        y_g = None
        for hl in range(HEADS_PER_GROUP):
            hd = g * HEADS_PER_GROUP + hl
            seg = a_cs[:, hd:hd + 1] - a_cs_t[hd:hd + 1, :]
            m = (cb * jnp.exp(jnp.where(tril, seg, NEG_BIG))).astype(BF16)
            in_head = (glane >= hl * SSM_HEAD_DIM) & (glane < (hl + 1) * SSM_HEAD_DIM)
            d = _dot(m, jnp.where(in_head, xd_g, jnp.zeros_like(xd_g)))
            y_g = d if y_g is None else y_g + d
        h_in = h_ref[g]
        y_off = _dot_nt(cg, h_in.astype(BF16)) * e_in[:, gs]
        ys.append(y_g + y_off)
        decay_rows = jnp.concatenate(
            [jnp.broadcast_to(e_chunk_t[g * HEADS_PER_GROUP + hl:g * HEADS_PER_GROUP + hl + 1, :],
                              (SSM_HEAD_DIM, SSM_STATE)) for hl in range(HEADS_PER_GROUP)], axis=0)
        h_ref[g] = h_in * decay_rows + _dot_tn(xw[:, gs], bg)

    y = jnp.concatenate(ys, axis=1) + dskip_ref[...] * xs
    y = y * _silu(z_ref[0])
    outs = []
    for g in range(SSM_GROUPS):
        gs = slice(g * GROUP_WIDTH, (g + 1) * GROUP_WIDTH)
        outs.append(_rms(y[:, gs], on_ref[:, gs]))
    y_ref[0] = jnp.concatenate(outs, axis=1).astype(BF16)


def _ssd(xbc, z, dt, cw, cb, alog, dskip, onorm):
    b, s, _ = xbc.shape
    blk = lambda w: pl.BlockSpec((1, CHUNK, w), lambda bi, ci: (bi, ci, 0))
    small = lambda r, w: pl.BlockSpec((r, w), lambda bi, ci: (0, 0))
    return pl.pallas_call(
        _ssd_kernel,
        grid=(b, s // CHUNK),
        in_specs=[blk(CONV_DIM), blk(SSM_WIDTH), blk(LANES), small(CONV_WIDTH, CONV_DIM),
                  small(1, CONV_DIM), small(1, LANES), small(1, SSM_WIDTH), small(1, SSM_WIDTH)],
        out_specs=blk(SSM_WIDTH),
        out_shape=jax.ShapeDtypeStruct((b, s, SSM_WIDTH), BF16),
        scratch_shapes=[pltpu.VMEM((SUBLANES + CHUNK, CONV_DIM), F32),
                        pltpu.VMEM((SSM_GROUPS, GROUP_WIDTH, SSM_STATE), F32)],
        compiler_params=pltpu.CompilerParams(dimension_semantics=("arbitrary", "arbitrary"),
                                             vmem_limit_bytes=VMEM_LIMIT_BYTES),
        name="ssd",
    )(xbc, z, dt, cw, cb, alog, dskip, onorm)


def _out_ffn_ple_kernel(x1_ref, oa_ref, ys_ref, p_ref, woa_ref, woy_ref, n2_ref, wg_ref, wu_ref,
                        wd_ref, ng_ref, wpg_ref, wpp_ref, npl_ref, out_ref, *, ff_chunk):
    x2 = x1_ref[...] + _dot(oa_ref[...], woa_ref[...]) + _dot(ys_ref[...], woy_ref[...])
    x3 = _swiglu_residual(x2, n2_ref[...], wg_ref, wu_ref, wd_ref, ff_chunk)
    e = _rms(_dot(p_ref[...].astype(BF16), wpp_ref[...]), npl_ref[...])
    gate = jax.nn.sigmoid(_dot(_rms(x3, ng_ref[...]).astype(BF16), wpg_ref[...]))
    out_ref[...] = x3 + gate * e


def _out_ffn_ple(x1, oa, ys, p, woa, woy, n2, wg, wu, wd, ng, wpg, wpp, npl, *, tm, ff_chunk):
    n, d = x1.shape
    d_ff = wg.shape[1]
    row = lambda w: pl.BlockSpec((tm, w), lambda i: (i, 0))
    return pl.pallas_call(
        functools.partial(_out_ffn_ple_kernel, ff_chunk=ff_chunk),
        grid=(n // tm,),
        in_specs=[row(d), row(ATTN_WIDTH), row(SSM_WIDTH), row(p.shape[1]),
                  _resident(woa.shape), _resident(woy.shape), _resident((1, d)),
                  _resident((d, d_ff)), _resident((d, d_ff)), _resident((d_ff, d)),
                  _resident((1, d)), _resident(wpg.shape), _resident(wpp.shape), _resident((1, d))],
        out_specs=row(d),
        out_shape=jax.ShapeDtypeStruct((n, d), F32),
        compiler_params=pltpu.CompilerParams(dimension_semantics=("arbitrary",),
                                             vmem_limit_bytes=VMEM_LIMIT_BYTES),
        name="out_ffn_ple",
    )(x1, oa, ys, p, woa, woy, n2, wg, wu, wd, ng, wpg, wpp, npl)


def _tiles(n, s, d_ff):
    tm = 512 if s % 512 == 0 else CHUNK
    tq = 512 if s % 512 == 0 else CHUNK
    tk = tq
    ff_chunk = 6 * MXU_WIDTH
    return tm, tq, tk, ff_chunk


def _layer(i, x, p_i, prm):
    b, s, d = x.shape
    n = b * s
    d_ff = prm["ffn1_w_gate"].shape[-1]
    tm, tq, tk, ff_chunk = _tiles(n, s, d_ff)
    lam_init = 0.8 - 0.6 * math.exp(-0.3 * i)
    row = lambda v: v.reshape(1, -1).astype(F32)
    w16 = lambda w: w.astype(BF16)

    w_in = prm["w_in"]
    n_main = 3 * ATTN_WIDTH + SSM_WIDTH + CONV_DIM
    win = w16(w_in[:, :n_main])
    wdt = w16(jnp.pad(w_in[:, n_main:], ((0, 0), (0, LANES - N_SSM_HEADS))))
    dtb = jnp.pad(row(prm["dt_bias"]), ((0, 0), (0, LANES - N_SSM_HEADS)))
    alog = jnp.pad(row(prm["a_log"]), ((0, 0), (0, LANES - N_SSM_HEADS)))
    qg = jnp.tile(row(prm["q_norm"]), (1, 2)) * (ATTN_HEAD_DIM ** -0.5 * LOG2E)
    kg = jnp.tile(row(prm["k_norm"]), (1, 2))
    dskip = jnp.repeat(row(prm["d_skip"]), SSM_HEAD_DIM, axis=1)
    w_out = prm["w_out"]

    x1, q, kaug, vt, z, xbc, dt = _ffn_inproj(
        x.reshape(n, d), row(prm["ffn1_norm"]), w16(prm["ffn1_w_gate"]), w16(prm["ffn1_w_up"]),
        w16(prm["ffn1_w_down"]), row(prm["mix_norm"]), win, wdt, qg, kg, dtb,
        seq_len=s, tm=tm, ff_chunk=ff_chunk)

    r3 = lambda t: t.reshape(b, s, t.shape[-1])
    oa = _diff_attn(r3(q), r3(kaug), vt, row(prm["lambda_q1"]), row(prm["lambda_k1"]),
                    row(prm["lambda_q2"]), row(prm["lambda_k2"]), row(prm["attn_out_norm"]),
                    tq=tq, tk=tk, lam_init=lam_init)
    ys = _ssd(r3(xbc), r3(z), r3(dt), prm["conv_w"].astype(F32), row(prm["conv_b"]), alog, dskip,
              row(prm["ssm_out_norm"]))

    out = _out_ffn_ple(
        x1, oa.reshape(n, ATTN_WIDTH), ys.reshape(n, SSM_WIDTH), p_i.reshape(n, p_i.shape[-1]),
        w16(w_out[:ATTN_WIDTH]), w16(w_out[ATTN_WIDTH:]), row(prm["ffn2_norm"]),
        w16(prm["ffn2_w_gate"]), w16(prm["ffn2_w_up"]), w16(prm["ffn2_w_down"]),
        row(prm["ple_gate_norm"]), w16(prm["w_ple_gate"]), w16(prm["w_ple_proj"]),
        row(prm["ple_norm"]), tm=tm, ff_chunk=ff_chunk)
    return out.reshape(b, s, d)


def kernel(x, p, ffn1_norm, ffn1_w_gate, ffn1_w_up, ffn1_w_down, mix_norm, w_in, q_norm, k_norm, lambda_q1, lambda_k1, lambda_q2, lambda_k2, attn_out_norm, conv_w, conv_b, dt_bias, a_log, d_skip, ssm_out_norm, w_out, ffn2_norm, ffn2_w_gate, ffn2_w_up, ffn2_w_down, ple_gate_norm, w_ple_gate, w_ple_proj, ple_norm):
    stacked = dict(
        ffn1_norm=ffn1_norm, ffn1_w_gate=ffn1_w_gate, ffn1_w_up=ffn1_w_up, ffn1_w_down=ffn1_w_down,
        mix_norm=mix_norm, w_in=w_in, q_norm=q_norm, k_norm=k_norm, lambda_q1=lambda_q1,
        lambda_k1=lambda_k1, lambda_q2=lambda_q2, lambda_k2=lambda_k2, attn_out_norm=attn_out_norm,
        conv_w=conv_w, conv_b=conv_b, dt_bias=dt_bias, a_log=a_log, d_skip=d_skip,
        ssm_out_norm=ssm_out_norm, w_out=w_out, ffn2_norm=ffn2_norm, ffn2_w_gate=ffn2_w_gate,
        ffn2_w_up=ffn2_w_up, ffn2_w_down=ffn2_w_down, ple_gate_norm=ple_gate_norm,
        w_ple_gate=w_ple_gate, w_ple_proj=w_ple_proj, ple_norm=ple_norm)
    for i in range(p.shape[0]):
        x = _layer(i, x, p[i], {name: w[i] for name, w in stacked.items()})
    return x
```
